```python
import math
import jax, jax.numpy as jnp
from jax import lax
import numpy as np

D_MODEL = 1024
BATCH = 16
SEQ = 4096
DEPTH = 1

S5_WIDTH = 512
S5_GROUP = 16
S5_GROUPS = S5_WIDTH // S5_GROUP
S5_STATE = 64
S5_DT_MIN = 0.001
S5_DT_MAX = 0.1

RW_WIDTH = 512
RW_HEAD = 64
RW_HEADS = RW_WIDTH // RW_HEAD
DECAY_LORA = 64
AAA_LORA = 64
GATE_LORA = 128
RW_COLS = 3 * RW_WIDTH + DECAY_LORA + AAA_LORA + GATE_LORA

N_BRANCH = 2
IN_COLS = S5_WIDTH + RW_COLS + N_BRANCH * D_MODEL

N_EXPERTS = 32
TOP_K = 4
D_EXPERT = D_MODEL
SWIGLU_ALPHA = 1.702
SWIGLU_LIMIT = 7.0
MOE_BLOCK = 128

RMS_EPS = 1e-5
GN_EPS = 64e-5
L2_EPS = 1e-12

kernel_name = "hybrid_s5_rwkv7_moe_block"


def rmsnorm(x, g):
    xf = x.astype(jnp.float32)
    y = xf * lax.rsqrt(jnp.mean(xf * xf, axis=-1, keepdims=True) + RMS_EPS)
    return (y * g.astype(jnp.float32)).astype(x.dtype)


def s5_scan(u, lam_re, lam_im, log_dt, b_re, b_im, c_re, c_im, d_skip):
    f32 = jnp.float32
    bsz, seq, _ = u.shape
    uf = u.astype(f32)
    lr, li = lam_re.astype(f32), lam_im.astype(f32)
    dt = jnp.exp(log_dt.astype(f32))[:, None]
    mag = jnp.exp(lr * dt)
    ab_re, ab_im = mag * jnp.cos(li * dt), mag * jnp.sin(li * dt)
    den = lr * lr + li * li
    nr, ni = ab_re - 1.0, ab_im
    f_re = (nr * lr + ni * li) / den
    f_im = (ni * lr - nr * li) / den
    br, bi = b_re.astype(f32), b_im.astype(f32)
    bb_re = f_re[..., None] * br - f_im[..., None] * bi
    bb_im = f_re[..., None] * bi + f_im[..., None] * br
    ug = uf.reshape(bsz, seq, S5_GROUPS, S5_GROUP)
    bu_re = jnp.einsum('bsgc,gpc->bsgp', ug, bb_re)
    bu_im = jnp.einsum('bsgc,gpc->bsgp', ug, bb_im)
    a_re = jnp.broadcast_to(ab_re, (1, seq, S5_GROUPS, S5_STATE))
    a_im = jnp.broadcast_to(ab_im, (1, seq, S5_GROUPS, S5_STATE))

    def combine(e1, e2):
        a1r, a1i, b1r, b1i = e1
        a2r, a2i, b2r, b2i = e2
        return (a1r * a2r - a1i * a2i,
                a1r * a2i + a1i * a2r,
                a2r * b1r - a2i * b1i + b2r,
                a2r * b1i + a2i * b1r + b2i)

    _, _, xr, xi = lax.associative_scan(combine, (a_re, a_im, bu_re, bu_im), axis=1)
    y = (jnp.einsum('bsgp,gcp->bsgc', xr, c_re.astype(f32))
         - jnp.einsum('bsgp,gcp->bsgc', xi, c_im.astype(f32)))
    return y.reshape(bsz, seq, S5_WIDTH) + d_skip.astype(f32) * uf


def rwkv7_recurrence(r, decay, k, v, a_vec, b_vec):
    bsz, _, nh, n = r.shape

    def step(state, inp):
        rt, wt, kt, vt, at, bt = inp
        sa = jnp.einsum('bhvk,bhk->bhv', state, at)
        state = (state * wt[:, :, None, :] + sa[..., None] * bt[:, :, None, :]
                 + vt[..., None] * kt[:, :, None, :])
        return state, jnp.einsum('bhvk,bhk->bhv', state, rt)

    s0 = jnp.zeros((bsz, nh, n, n), jnp.float32)
    xs = (jnp.moveaxis(r, 1, 0), jnp.moveaxis(decay, 1, 0), jnp.moveaxis(k, 1, 0),
          jnp.moveaxis(v, 1, 0), jnp.moveaxis(a_vec, 1, 0), jnp.moveaxis(b_vec, 1, 0))
    _, ys = lax.scan(step, s0, xs)
    return jnp.moveaxis(ys, 0, 1)


def rwkv7_branch(p, mu, w0, w2, a0, a2, g2, k_k, k_a, r_k, ln_g, ln_b, w_o):
    f32 = jnp.float32
    bsz, seq, _ = p.shape
    pf = p.astype(f32)
    prev = jnp.pad(pf[:, :-1], ((0, 0), (1, 0), (0, 0)))
    xs = pf + mu.astype(f32) * (prev - pf)
    c1, c2, c3 = RW_WIDTH, 2 * RW_WIDTH, 3 * RW_WIDTH
    c4, c5 = c3 + DECAY_LORA, c3 + DECAY_LORA + AAA_LORA
    r, k, v = xs[..., :c1], xs[..., c1:c2], xs[..., c2:c3]
    wd, ad, gd = xs[..., c3:c4], xs[..., c4:c5], xs[..., c5:]
    w_log = -jax.nn.softplus(-(w0.astype(f32) + jnp.tanh(wd) @ w2.astype(f32))) - 0.5
    decay = jnp.exp(-jnp.exp(w_log))
    a = jax.nn.sigmoid(a0.astype(f32) + ad @ a2.astype(f32))
    g = jax.nn.sigmoid(gd) @ g2.astype(f32)
    heads = lambda t: t.reshape(bsz, seq, RW_HEADS, RW_HEAD)
    kk = heads(k * k_k.astype(f32))
    kk = kk / jnp.maximum(jnp.sqrt(jnp.sum(kk * kk, axis=-1, keepdims=True)), L2_EPS)
    k = k * (1.0 + (a - 1.0) * k_a.astype(f32))
    rh, kh, vh, ah = heads(r), heads(k), heads(v), heads(a)
    y = rwkv7_recurrence(rh, heads(decay), kh, vh, -kk, kk * ah)
    mean = jnp.mean(y, axis=-1, keepdims=True)
    var = jnp.mean(jnp.square(y - mean), axis=-1, keepdims=True)
    y = ((y - mean) * lax.rsqrt(var + GN_EPS)).reshape(bsz, seq, RW_WIDTH)
    y = y * ln_g.astype(f32) + ln_b.astype(f32)
    bonus = jnp.sum(rh * kh * r_k.astype(f32), axis=-1, keepdims=True) * vh
    y = (y + bonus.reshape(bsz, seq, RW_WIDTH)) * g
    return y @ w_o.astype(f32)


def mixer_sublayer(h, norm_g, w_in, lam_re, lam_im, log_dt, b_re, b_im, c_re, c_im, d_skip,
                   w_glu, b_glu, mu, w0, w2, a0, a2, g2, k_k, k_a, r_k, ln_g, ln_b, w_rw_out, w_out):
    f32 = jnp.float32
    xn = rmsnorm(h, norm_g)
    p = xn @ w_in
    u_s5 = p[..., :S5_WIDTH]
    p_rw = p[..., S5_WIDTH:S5_WIDTH + RW_COLS]
    gl = p[..., S5_WIDTH + RW_COLS:].astype(f32)
    ys = jax.nn.gelu(s5_scan(u_s5, lam_re, lam_im, log_dt, b_re, b_im, c_re, c_im, d_skip))
    z = ys @ w_glu.astype(f32) + b_glu.astype(f32)
    s5_out = z[..., :D_MODEL] * jax.nn.sigmoid(z[..., D_MODEL:])
    rw_out = rwkv7_branch(p_rw, mu, w0, w2, a0, a2, g2, k_k, k_a, r_k, ln_g, ln_b, w_rw_out)
    mixed = jax.nn.sigmoid(gl[..., :D_MODEL]) * s5_out + jax.nn.sigmoid(gl[..., D_MODEL:]) * rw_out
    return h + (mixed @ w_out.astype(f32)).astype(h.dtype)


def clamped_swiglu(hid):
    x_glu = jnp.minimum(hid[..., ::2], SWIGLU_LIMIT)
    x_lin = jnp.clip(hid[..., 1::2], -SWIGLU_LIMIT, SWIGLU_LIMIT)
    return x_glu * jax.nn.sigmoid(SWIGLU_ALPHA * x_glu) * (x_lin + 1.0)


def moe_sublayer(h, norm_g, router_w, router_b, w1, b1, w2, b2):
    bsz, seq, d = h.shape
    xn = rmsnorm(h, norm_g).reshape(-1, d)
    t = xn.shape[0]
    logits = xn.astype(jnp.float32) @ router_w.astype(jnp.float32) + router_b.astype(jnp.float32)
    top_vals, top_idx = lax.top_k(logits, TOP_K)
    gates = jax.nn.softmax(top_vals, axis=-1)
    n_assign = t * TOP_K
    n_blk = (n_assign + MOE_BLOCK - 1) // MOE_BLOCK + N_EXPERTS
    e_flat = top_idx.reshape(-1)
    tok_flat = jnp.arange(n_assign, dtype=jnp.int32) // TOP_K
    g_flat = gates.reshape(-1)
    order = jnp.argsort(e_flat, stable=True)
    e_s, tok_s, g_s = e_flat[order], tok_flat[order], g_flat[order]
    counts = jnp.bincount(e_flat, length=N_EXPERTS)
    offs = jnp.cumsum(counts) - counts
    pcounts = ((counts + MOE_BLOCK - 1) // MOE_BLOCK) * MOE_BLOCK
    pend = jnp.cumsum(pcounts)
    poffs = pend - pcounts
    dest = poffs[e_s] + (jnp.arange(n_assign, dtype=jnp.int32) - offs[e_s])
    row_tok = jnp.full((n_blk * MOE_BLOCK,), t, jnp.int32).at[dest].set(tok_s)
    row_g = jnp.zeros((n_blk * MOE_BLOCK,), jnp.float32).at[dest].set(g_s)
    blk_exp = jnp.minimum(jnp.searchsorted(pend, jnp.arange(n_blk) * MOE_BLOCK, side='right'),
                          N_EXPERTS - 1).astype(jnp.int32)
    x_pad = jnp.concatenate([xn, jnp.zeros((1, d), xn.dtype)], axis=0)

    def body(acc, inp):
        toks, gs, e = inp
        xb = x_pad[toks].astype(jnp.float32)
        hid = xb @ w1[e].astype(jnp.float32) + b1[e].astype(jnp.float32)
        out = (clamped_swiglu(hid) @ w2[e].astype(jnp.float32) + b2[e].astype(jnp.float32)) * gs[:, None]
        return acc.at[toks].add(out), None

    acc0 = jnp.zeros((t + 1, d), jnp.float32)
    acc, _ = lax.scan(body, acc0, (row_tok.reshape(n_blk, MOE_BLOCK),
                                   row_g.reshape(n_blk, MOE_BLOCK), blk_exp))
    return h + acc[:t].reshape(bsz, seq, d).astype(h.dtype)


def setup_inputs(seed: int = 0) -> dict:
    key = jax.random.key(seed)
    ks = jax.random.split(key, 40)
    f32 = jnp.float32
    nrm = lambda i, shape, s: jax.random.normal(ks[i], shape, f32) * s
    L, G, P = DEPTH, S5_GROUPS, S5_STATE
    return {
        "x": nrm(0, (BATCH, SEQ, D_MODEL), 1.0),
        "norm1_g": 1.0 + nrm(1, (L, D_MODEL), 0.02),
        "w_in": nrm(2, (L, D_MODEL, IN_COLS), D_MODEL ** -0.5),
        "s5_lambda_re": -0.5 + nrm(3, (L, G, P), 0.01),
        "s5_lambda_im": math.pi * jnp.arange(P, dtype=f32) + nrm(4, (L, G, P), 0.01),
        "s5_log_dt": jax.random.uniform(ks[5], (L, G), f32, math.log(S5_DT_MIN), math.log(S5_DT_MAX)),
        "s5_b_re": nrm(6, (L, G, P, S5_GROUP), (2 * S5_GROUP) ** -0.5),
        "s5_b_im": nrm(7, (L, G, P, S5_GROUP), (2 * S5_GROUP) ** -0.5),
        "s5_c_re": nrm(8, (L, G, S5_GROUP, P), P ** -0.5),
        "s5_c_im": nrm(9, (L, G, S5_GROUP, P), P ** -0.5),
        "s5_d": nrm(10, (L, S5_WIDTH), 1.0),
        "s5_w_glu": nrm(11, (L, S5_WIDTH, 2 * D_MODEL), S5_WIDTH ** -0.5),
        "s5_b_glu": nrm(12, (L, 2 * D_MODEL), 0.01),
        "rw_mu": jax.random.uniform(ks[13], (L, RW_COLS), f32, 0.0, 1.0),
        "rw_w0": jax.random.uniform(ks[14], (L, RW_WIDTH), f32, -5.0, 1.0),
        "rw_w2": nrm(15, (L, DECAY_LORA, RW_WIDTH), 0.5 * DECAY_LORA ** -0.5),
        "rw_a0": nrm(16, (L, RW_WIDTH), 0.5),
        "rw_a2": nrm(17, (L, AAA_LORA, RW_WIDTH), 0.5 * AAA_LORA ** -0.5),
        "rw_g2": nrm(18, (L, GATE_LORA, RW_WIDTH), GATE_LORA ** -0.5),
        "rw_k_k": 0.85 + nrm(19, (L, RW_WIDTH), 0.05),
        "rw_k_a": 1.0 + nrm(20, (L, RW_WIDTH), 0.05),
        "rw_r_k": nrm(21, (L, RW_HEADS, RW_HEAD), 0.1),
        "rw_ln_g": 1.0 + nrm(22, (L, RW_WIDTH), 0.02),
        "rw_ln_b": nrm(23, (L, RW_WIDTH), 0.02),
        "rw_w_out": nrm(24, (L, RW_WIDTH, D_MODEL), RW_WIDTH ** -0.5),
        "w_out": nrm(25, (L, D_MODEL, D_MODEL), D_MODEL ** -0.5),
        "norm2_g": 1.0 + nrm(26, (L, D_MODEL), 0.02),
        "router_w": nrm(27, (L, D_MODEL, N_EXPERTS), D_MODEL ** -0.5),
        "router_b": nrm(28, (L, N_EXPERTS), 0.01),
        "moe_w1": nrm(29, (L, N_EXPERTS, D_MODEL, 2 * D_EXPERT), D_MODEL ** -0.5),
        "moe_b1": nrm(30, (L, N_EXPERTS, 2 * D_EXPERT), 0.01),
        "moe_w2": nrm(31, (L, N_EXPERTS, D_EXPERT, D_MODEL), D_EXPERT ** -0.5),
        "moe_b2": nrm(32, (L, N_EXPERTS, D_MODEL), 0.01),
        "norm_f_g": 1.0 + nrm(33, (D_MODEL,), 0.02),
    }


def reference(x, norm1_g, w_in, s5_lambda_re, s5_lambda_im, s5_log_dt, s5_b_re, s5_b_im,
              s5_c_re, s5_c_im, s5_d, s5_w_glu, s5_b_glu, rw_mu, rw_w0, rw_w2, rw_a0, rw_a2,
              rw_g2, rw_k_k, rw_k_a, rw_r_k, rw_ln_g, rw_ln_b, rw_w_out, w_out, norm2_g,
              router_w, router_b, moe_w1, moe_b1, moe_w2, moe_b2, norm_f_g):
    h = x
    for l in range(DEPTH):
        h = mixer_sublayer(h, norm1_g[l], w_in[l], s5_lambda_re[l], s5_lambda_im[l], s5_log_dt[l],
                           s5_b_re[l], s5_b_im[l], s5_c_re[l], s5_c_im[l], s5_d[l], s5_w_glu[l],
                           s5_b_glu[l], rw_mu[l], rw_w0[l], rw_w2[l], rw_a0[l], rw_a2[l], rw_g2[l],
                           rw_k_k[l], rw_k_a[l], rw_r_k[l], rw_ln_g[l], rw_ln_b[l], rw_w_out[l],
                           w_out[l])
        h = moe_sublayer(h, norm2_g[l], router_w[l], router_b[l], moe_w1[l], moe_b1[l],
                         moe_w2[l], moe_b2[l])
    return rmsnorm(h, norm_f_g)
```

```python
import functools
import math

import jax
import jax.numpy as jnp
from jax import lax
from jax.experimental import pallas as pl
from jax.experimental.pallas import tpu as pltpu

F32 = jnp.float32
BF16 = jnp.bfloat16
I32 = jnp.int32

D_MODEL = 1024
S5_WIDTH = 512
S5_GROUP = 16
S5_GROUPS = 32
S5_STATE = 64
S5_HALF_GROUPS = 16
S5_HALF_IN = S5_HALF_GROUPS * S5_GROUP
S5_HALF_ST = S5_HALF_GROUPS * S5_STATE
S5_ST_COLS = 4 * S5_HALF_ST
RW_WIDTH = 512
RW_HEAD = 64
RW_HEADS = 8
RW_COLS = 1792
IN_COLS = 4352
N_EXPERTS = 32
TOP_K = 4
SWIGLU_ALPHA = 1.702
SWIGLU_LIMIT = 7.0
RMS_EPS = 1e-5
GN_EPS = 64e-5
L2_EPS = 1e-12

VMEM_LIMIT_BYTES = 48 * 1024 * 1024

IN_TILE = 256
S5_STEPS = 32
S5_LANES = 512
RW_CHUNK = 64
ROUTE_TILE = 512
MOE_ROWS = 256
DISPATCH_TILE = 256
COMBINE_TILE = 128


def _cparams(*sem):
    return pltpu.CompilerParams(dimension_semantics=sem, vmem_limit_bytes=VMEM_LIMIT_BYTES)


def _dot(a, b):
    return jnp.dot(a.astype(BF16), b.astype(BF16), preferred_element_type=F32)


def _dot_nt(a, b):
    return lax.dot_general(a.astype(BF16), b.astype(BF16), (((1,), (1,)), ((), ())),
                           preferred_element_type=F32)


def _dot_tn(a, b):
    return lax.dot_general(a.astype(BF16), b.astype(BF16), (((0,), (0,)), ((), ())),
                           preferred_element_type=F32)


def _dot_split(a, b_f32):
    hi = b_f32.astype(BF16)
    lo = (b_f32 - hi.astype(F32)).astype(BF16)
    return (jnp.dot(a, hi, preferred_element_type=F32)
            + jnp.dot(a, lo, preferred_element_type=F32))


def _split_dot(a_f32, b):
    hi = a_f32.astype(BF16)
    lo = (a_f32 - hi.astype(F32)).astype(BF16)
    return (jnp.dot(hi, b, preferred_element_type=F32)
            + jnp.dot(lo, b, preferred_element_type=F32))


def _in_proj_kernel(x_ref, g_ref, w_ref, u_ref, prw_ref, gs5_ref, grw_ref):
    x = x_ref[...]
    xn = x * lax.rsqrt(jnp.mean(x * x, axis=-1, keepdims=True) + RMS_EPS) * g_ref[...]
    p = _dot(xn, w_ref[...])
    c0, c1, c2 = S5_WIDTH, S5_WIDTH + RW_COLS, S5_WIDTH + RW_COLS + D_MODEL
    u_ref[...] = p[:, :c0].astype(BF16)
    prw_ref[...] = p[:, c0:c1]
    gs5_ref[...] = jax.nn.sigmoid(p[:, c1:c2]).astype(BF16)
    grw_ref[...] = jax.nn.sigmoid(p[:, c2:]).astype(BF16)


def _in_proj(x2d, norm_g, w_in_bf, bsz, seq):
    ts = IN_TILE
    ns = seq // ts
    row = lambda b, s: (b * ns + s, 0)
    tm = lambda b, s: (s, b)
    return pl.pallas_call(
        _in_proj_kernel,
        grid=(bsz, ns),
        in_specs=[pl.BlockSpec((ts, D_MODEL), row),
                  pl.BlockSpec((1, D_MODEL), lambda b, s: (0, 0)),
                  pl.BlockSpec((D_MODEL, IN_COLS), lambda b, s: (0, 0))],
        out_specs=[pl.BlockSpec((ts, S5_WIDTH), tm),
                   pl.BlockSpec((ts, RW_COLS), row),
                   pl.BlockSpec((ts, D_MODEL), tm),
                   pl.BlockSpec((ts, D_MODEL), row)],
        out_shape=[jax.ShapeDtypeStruct((seq, bsz * S5_WIDTH), BF16),
                   jax.ShapeDtypeStruct((bsz * seq, RW_COLS), F32),
                   jax.ShapeDtypeStruct((seq, bsz * D_MODEL), BF16),
                   jax.ShapeDtypeStruct((bsz * seq, D_MODEL), BF16)],
        compiler_params=_cparams("parallel", "parallel"),
        name="in_proj",
    )(x2d, norm_g, w_in_bf)


def _s5_kernel(bsz, u_ref, gs5_ref, a_ref, bd_ref, cd_ref, d_ref, wg_ref, bg_ref,
               out_ref, st_ref, xb_ref):
    @pl.when(pl.program_id(0) == 0)
    def _():
        st_ref[...] = jnp.zeros_like(st_ref)

    u = u_ref[...]
    for h in range(2):
        xb_ref[:, 2 * S5_HALF_ST * h:2 * S5_HALF_ST * (h + 1)] = jnp.dot(
            u[:, S5_HALF_IN * h:S5_HALF_IN * (h + 1)], bd_ref[h], preferred_element_type=F32)

    for h in range(2):
        for j in range(S5_HALF_ST // S5_LANES):
            ro = 2 * S5_HALF_ST * h + S5_LANES * j
            io = ro + S5_HALF_ST
            ar = jnp.broadcast_to(a_ref[:, ro:ro + S5_LANES], (bsz, S5_LANES))
            ai = jnp.broadcast_to(a_ref[:, io:io + S5_LANES], (bsz, S5_LANES))

            def step(t, carry, ro=ro, io=io, ar=ar, ai=ai):
                xr, xi = carry
                r0 = pl.multiple_of(t * bsz, bsz)
                nr = ar * xr - ai * xi + xb_ref[pl.ds(r0, bsz), ro:ro + S5_LANES]
                ni = ar * xi + ai * xr + xb_ref[pl.ds(r0, bsz), io:io + S5_LANES]
                xb_ref[pl.ds(r0, bsz), ro:ro + S5_LANES] = nr
                xb_ref[pl.ds(r0, bsz), io:io + S5_LANES] = ni
                return nr, ni

            xr, xi = lax.fori_loop(0, S5_STEPS, step,
                                   (st_ref[:, ro:ro + S5_LANES], st_ref[:, io:io + S5_LANES]))
            st_ref[:, ro:ro + S5_LANES] = xr
            st_ref[:, io:io + S5_LANES] = xi

    ys = []
    for h in range(2):
        xh = xb_ref[:, 2 * S5_HALF_ST * h:2 * S5_HALF_ST * (h + 1)]
        ys.append(jnp.dot(xh.astype(BF16), cd_ref[h], preferred_element_type=F32))
    y = jnp.concatenate(ys, axis=1) + d_ref[...] * u.astype(F32)
    y = jax.nn.gelu(y)
    z = _dot(y, wg_ref[...]) + bg_ref[...]
    s5 = z[:, :D_MODEL] * jax.nn.sigmoid(z[:, D_MODEL:])
    out_ref[...] = (s5 * gs5_ref[...].astype(F32)).astype(BF16)


def _s5_params(lam_re, lam_im, log_dt, b_re, b_im, c_re, c_im):
    dt = jnp.exp(log_dt)[:, None]
    mag = jnp.exp(lam_re * dt)
    ab_re, ab_im = mag * jnp.cos(lam_im * dt), mag * jnp.sin(lam_im * dt)
    den = lam_re * lam_re + lam_im * lam_im
    nr, ni = ab_re - 1.0, ab_im
    f_re = (nr * lam_re + ni * lam_im) / den
    f_im = (ni * lam_re - nr * lam_im) / den
    bb_re = f_re[..., None] * b_re - f_im[..., None] * b_im
    bb_im = f_re[..., None] * b_im + f_im[..., None] * b_re
    eye = jnp.eye(S5_HALF_GROUPS, dtype=F32)

    def bd_half(bb):
        return jnp.einsum('gpc,gh->gchp', bb, eye).reshape(S5_HALF_IN, S5_HALF_ST)

    def cd_half(cc):
        return jnp.einsum('gcp,gh->gphc', cc, eye).reshape(S5_HALF_ST, S5_HALF_IN)

    bd, cd, a = [], [], []
    for h in range(2):
        sl = slice(S5_HALF_GROUPS * h, S5_HALF_GROUPS * (h + 1))
        bd.append(jnp.concatenate([bd_half(bb_re[sl]), bd_half(bb_im[sl])], axis=1))
        cd.append(jnp.concatenate([cd_half(c_re[sl]), cd_half(-c_im[sl])], axis=0))
        a += [ab_re[sl].reshape(1, S5_HALF_ST), ab_im[sl].reshape(1, S5_HALF_ST)]
    return (jnp.concatenate(a, axis=1), jnp.stack(bd).astype(BF16), jnp.stack(cd).astype(BF16))


def _s5_branch(u_tm, gs5_tm, a_vec, bd, cd, d_skip, w_glu_bf, b_glu, bsz, seq):
    rows = S5_STEPS * bsz
    const2 = lambda i: (0, 0)
    const3 = lambda i: (0, 0, 0)
    return pl.pallas_call(
        functools.partial(_s5_kernel, bsz),
        grid=(seq // S5_STEPS,),
        in_specs=[pl.BlockSpec((rows, S5_WIDTH), lambda i: (i, 0)),
                  pl.BlockSpec((rows, D_MODEL), lambda i: (i, 0)),
                  pl.BlockSpec((1, S5_ST_COLS), const2),
                  pl.BlockSpec((2, S5_HALF_IN, 2 * S5_HALF_ST), const3),
                  pl.BlockSpec((2, 2 * S5_HALF_ST, S5_HALF_IN), const3),
                  pl.BlockSpec((1, S5_WIDTH), const2),
                  pl.BlockSpec((S5_WIDTH, 2 * D_MODEL), const2),
                  pl.BlockSpec((1, 2 * D_MODEL), const2)],
        out_specs=pl.BlockSpec((rows, D_MODEL), lambda i: (i, 0)),
        out_shape=jax.ShapeDtypeStruct((seq * bsz, D_MODEL), BF16),
        scratch_shapes=[pltpu.VMEM((bsz, S5_ST_COLS), F32),
                        pltpu.VMEM((rows, S5_ST_COLS), F32)],
        compiler_params=_cparams("arbitrary"),
        name="s5_branch",
    )(u_tm, gs5_tm, a_vec, bd, cd, d_skip, w_glu_bf, b_glu)


def _rwkv_kernel(p_ref, grw_ref, mu_ref, w0_ref, w2_ref, a0_ref, a2_ref, g2_ref, kk_ref,
                 ka_ref, rk_ref, lng_ref, lnb_ref, wo_ref, hsum_ref, tri_ref,
                 out_ref, st_ref, prev_ref):
    L, N, H = RW_CHUNK, RW_HEAD, RW_HEADS

    @pl.when(pl.program_id(1) == 0)
    def _():
        st_ref[...] = jnp.zeros_like(st_ref)
        prev_ref[...] = jnp.zeros_like(prev_ref)

    p = p_ref[...]
    first = lax.broadcasted_iota(I32, (L, 1), 0) == 0
    prev = jnp.where(first, prev_ref[...], pltpu.roll(p, 1, axis=0))
    prev_ref[...] = p[L - 1:L, :]
    xs = p + mu_ref[...] * (prev - p)
    c1, c2, c3 = RW_WIDTH, 2 * RW_WIDTH, 3 * RW_WIDTH
    r, k, v = xs[:, :c1], xs[:, c1:c2], xs[:, c2:c3]
    lora = xs[:, c3:c3 + 128]
    gd = xs[:, c3 + 128:]
    hsum = hsum_ref[...]

    w_log = -jax.nn.softplus(-(w0_ref[...] + _dot(jnp.tanh(lora), w2_ref[...]))) - 0.5
    e = jnp.exp(w_log)
    a = jax.nn.sigmoid(a0_ref[...] + _dot(lora, a2_ref[...]))
    g = _dot(jax.nn.sigmoid(gd), g2_ref[...])
    kk = k * kk_ref[...]
    kk = kk / jnp.maximum(jnp.sqrt(_split_dot(kk * kk, hsum)), L2_EPS)
    k = k * (1.0 + (a - 1.0) * ka_ref[...])
    avec, bvec = -kk, kk * a

    cum = _dot_split(tri_ref[...], e)
    cum_last = cum[L - 1:L, :]
    dec_in = jnp.exp(-cum)
    at = avec * jnp.exp(e - cum)
    rt = r * dec_in
    grow = jnp.exp(cum)
    bt, kt = bvec * grow, k * grow
    tail = jnp.exp(cum - cum_last)
    bw, kw = bvec * tail, k * tail
    w_all = jnp.exp(-cum_last)

    ti = lax.broadcasted_iota(I32, (L, 2 * L), 0)
    si = lax.broadcasted_iota(I32, (L, 2 * L), 1)
    m_ab = jnp.where((si < L) & (si < ti), 1.0, 0.0)
    m_ak = jnp.where((si >= L) & (si - L < ti), 1.0, 0.0)
    m_rr = jnp.where(jnp.where(si >= L, si - L, si) <= ti, 1.0, 0.0)

    ys = []
    for h in range(H):
        hs = slice(N * h, N * (h + 1))
        vh = v[:, hs]
        s_old = st_ref[h]
        ar_ = jnp.concatenate([at[:, hs], rt[:, hs]], axis=0)
        bk_ = jnp.concatenate([bt[:, hs], kt[:, hs]], axis=0)
        pm = _dot_nt(ar_, bk_)
        ars = _dot_nt(ar_, s_old)
        top, bot = pm[:L], pm[L:]
        n_pow = (top * m_ab)[:, :L]
        x = ars[:L] + _dot(top * m_ak, jnp.concatenate([vh, vh], axis=0))
        for i in range(6):
            x = x + _dot(n_pow, x)
            if i < 5:
                n_pow = _dot(n_pow, n_pow)
        uv = jnp.concatenate([x, vh], axis=0)
        ys.append(ars[L:] + _dot(bot * m_rr, uv))
        bkw = jnp.concatenate([bw[:, hs], kw[:, hs]], axis=0)
        st_ref[h] = s_old * w_all[:, hs] + _dot_tn(uv, bkw)
    y = jnp.concatenate(ys, axis=1)

    inv_n = 1.0 / N
    mean = _split_dot(y, hsum) * inv_n
    dlt = y - mean
    var = _split_dot(dlt * dlt, hsum) * inv_n
    y = dlt * lax.rsqrt(var + GN_EPS) * lng_ref[...] + lnb_ref[...]
    bonus = _split_dot(r * k * rk_ref[...], hsum) * v
    y = (y + bonus) * g
    out_ref[...] = (_dot(y, wo_ref[...]) * grw_ref[...].astype(F32)).astype(BF16)


def _rwkv_branch(prw, grw, mu, w0, w2p, a0, a2p, g2, k_k, k_a, r_k, ln_g, ln_b, wo_bf, bsz, seq):
    L = RW_CHUNK
    nc = seq // L
    row = lambda b, c: (b * nc + c, 0)
    const = lambda b, c: (0, 0)
    hsum = jnp.kron(jnp.eye(RW_HEADS, dtype=F32), jnp.ones((RW_HEAD, RW_HEAD), F32)).astype(BF16)
    tri = jnp.tril(jnp.ones((L, L), F32)).astype(BF16)
    vec = lambda n: pl.BlockSpec((1, n), const)
    return pl.pallas_call(
        _rwkv_kernel,
        grid=(bsz, nc),
        in_specs=[pl.BlockSpec((L, RW_COLS), row),
                  pl.BlockSpec((L, D_MODEL), row),
                  vec(RW_COLS), vec(RW_WIDTH),
                  pl.BlockSpec((128, RW_WIDTH), const),
                  vec(RW_WIDTH),
                  pl.BlockSpec((128, RW_WIDTH), const),
                  pl.BlockSpec((128, RW_WIDTH), const),
                  vec(RW_WIDTH), vec(RW_WIDTH), vec(RW_WIDTH), vec(RW_WIDTH), vec(RW_WIDTH),
                  pl.BlockSpec((RW_WIDTH, D_MODEL), const),
                  pl.BlockSpec((RW_WIDTH, RW_WIDTH), const),
                  pl.BlockSpec((L, L), const)],
        out_specs=pl.BlockSpec((L, D_MODEL), row),
        out_shape=jax.ShapeDtypeStruct((bsz * seq, D_MODEL), BF16),
        scratch_shapes=[pltpu.VMEM((RW_HEADS, RW_HEAD, RW_HEAD), F32),
                        pltpu.VMEM((1, RW_COLS), F32)],
        compiler_params=_cparams("arbitrary", "arbitrary"),
        name="rwkv_branch",
    )(prw, grw, mu, w0, w2p, a0, a2p, g2, k_k, k_a, r_k, ln_g, ln_b, wo_bf, hsum, tri)


def _mix_kernel(x_ref, s5_ref, rw_ref, wout_ref, g2_ref, rwt_ref, rb_ref,
                h_ref, xn_ref, lg_ref):
    mixed = s5_ref[...].astype(F32) + rw_ref[...].astype(F32)
    h = x_ref[...] + _dot(mixed, wout_ref[...])
    h_ref[...] = h
    xn = h * lax.rsqrt(jnp.mean(h * h, axis=-1, keepdims=True) + RMS_EPS) * g2_ref[...]
    xn_ref[...] = xn
    lg_ref[...] = lax.dot_general(rwt_ref[...], xn, (((1,), (1,)), ((), ())),
                                  precision=lax.Precision.HIGHEST,
                                  preferred_element_type=F32) + rb_ref[...]


def _mix(x2d, s5_tm, rwg, w_out_bf, norm2_g, router_wt, router_b, bsz, seq):
    ts = IN_TILE
    ns = seq // ts
    row = lambda b, s: (b * ns + s, 0)
    const = lambda b, s: (0, 0)
    t = bsz * seq
    return pl.pallas_call(
        _mix_kernel,
        grid=(bsz, ns),
        in_specs=[pl.BlockSpec((ts, D_MODEL), row),
                  pl.BlockSpec((ts, D_MODEL), lambda b, s: (s, b)),
                  pl.BlockSpec((ts, D_MODEL), row),
                  pl.BlockSpec((D_MODEL, D_MODEL), const),
                  pl.BlockSpec((1, D_MODEL), const),
                  pl.BlockSpec((N_EXPERTS, D_MODEL), const),
                  pl.BlockSpec((N_EXPERTS, 1), const)],
        out_specs=[pl.BlockSpec((ts, D_MODEL), row),
                   pl.BlockSpec((ts, D_MODEL), row),
                   pl.BlockSpec((N_EXPERTS, ts), lambda b, s: (0, b * ns + s))],
        out_shape=[jax.ShapeDtypeStruct((t, D_MODEL), F32),
                   jax.ShapeDtypeStruct((t, D_MODEL), F32),
                   jax.ShapeDtypeStruct((N_EXPERTS, t), F32)],
        compiler_params=_cparams("parallel", "parallel"),
        name="mix_out_proj",
    )(x2d, s5_tm, rwg, w_out_bf, norm2_g, router_wt, router_b)


def _route_kernel(lg_ref, triu_ref, route_ref, gate_ref, cnt_ref, run_ref):
    @pl.when(pl.program_id(0) == 0)
    def _():
        run_ref[...] = jnp.zeros_like(run_ref)

    lg = lg_ref[...]
    tr = lg.shape[1]
    ie = lax.broadcasted_iota(I32, lg.shape, 0)
    sel = jnp.zeros(lg.shape, F32)
    hots, vals, idxs = [], [], []
    for _ in range(TOP_K):
        m = jnp.max(lg, axis=0, keepdims=True)
        idx = jnp.min(jnp.where(lg == m, ie, N_EXPERTS), axis=0, keepdims=True)
        hot = ie == idx
        hots.append(hot)
        vals.append(m)
        idxs.append(idx)
        sel = jnp.where(hot, 1.0, sel)
        lg = jnp.where(hot, -jnp.inf, lg)
    ex = [jnp.exp(vk - vals[0]) for vk in vals]
    den = ex[0] + ex[1] + ex[2] + ex[3]
    before = jnp.dot(sel.astype(BF16), triu_ref[...], preferred_element_type=F32) + run_ref[:, 0:1]
    ranks = [jnp.sum(jnp.where(hot, before, 0.0), axis=0, keepdims=True) for hot in hots]
    run = run_ref[...] + jnp.sum(sel, axis=1, keepdims=True)
    run_ref[...] = run
    cnt_ref[...] = run.astype(I32)
    route_ref[...] = jnp.concatenate(idxs + [rk.astype(I32) for rk in ranks], axis=0)
    gate_ref[...] = jnp.concatenate([ek / den for ek in ex] + [jnp.zeros((4, tr), F32)], axis=0)


def _route(logits_t):
    t = logits_t.shape[1]
    tr = ROUTE_TILE
    triu = jnp.triu(jnp.ones((tr, tr), F32), k=1).astype(BF16)
    return pl.pallas_call(
        _route_kernel,
        grid=(t // tr,),
        in_specs=[pl.BlockSpec((N_EXPERTS, tr), lambda i: (0, i)),
                  pl.BlockSpec((tr, tr), lambda i: (0, 0))],
        out_specs=[pl.BlockSpec((8, tr), lambda i: (0, i)),
                   pl.BlockSpec((8, tr), lambda i: (0, i)),
                   pl.BlockSpec((N_EXPERTS, 128), lambda i: (0, 0))],
        out_shape=[jax.ShapeDtypeStruct((8, t), I32),
                   jax.ShapeDtypeStruct((8, t), F32),
                   jax.ShapeDtypeStruct((N_EXPERTS, 128), I32)],
        scratch_shapes=[pltpu.VMEM((N_EXPERTS, 128), F32)],
        compiler_params=_cparams("arbitrary"),
        name="route",
    )(logits_t, triu)


def _row_copy(src, dst, sem):
    return pltpu.make_async_copy(src, dst, sem)


def _dispatch_kernel(cnt_ref, pcnt_ref, poff_ref, xn_ref, route_hbm, xs_hbm,
                     route_smem, zero_ref, sem_idx, sem_rows):
    i = pl.program_id(0)
    td = xn_ref.shape[0]

    idx_copy = pltpu.make_async_copy(route_hbm.at[:, pl.ds(i * td, td)], route_smem, sem_idx)
    idx_copy.start()

    @pl.when(i == 0)
    def _():
        zero_ref[...] = jnp.zeros_like(zero_ref)

        def per_expert(e, carry):
            base = poff_ref[e]

            def start(rw, c):
                _row_copy(zero_ref, xs_hbm.at[pl.ds(base + rw, 1)], sem_rows).start()
                return c

            def wait(rw, c):
                _row_copy(zero_ref, xs_hbm.at[pl.ds(base + rw, 1)], sem_rows).wait()
                return c

            lax.fori_loop(cnt_ref[e], pcnt_ref[e], start, 0)
            lax.fori_loop(cnt_ref[e], pcnt_ref[e], wait, 0)
            return carry

        lax.fori_loop(0, N_EXPERTS, per_expert, 0)

    idx_copy.wait()

    def start(tk, c):
        for kk in range(TOP_K):
            d = poff_ref[route_smem[kk, tk]] + route_smem[TOP_K + kk, tk]
            _row_copy(xn_ref.at[pl.ds(tk, 1)], xs_hbm.at[pl.ds(d, 1)], sem_rows).start()
        return c

    def wait(tk, c):
        for kk in range(TOP_K):
            _row_copy(xn_ref.at[pl.ds(0, 1)], xs_hbm.at[pl.ds(0, 1)], sem_rows).wait()
        return c

    lax.fori_loop(0, td, start, 0)
    lax.fori_loop(0, td, wait, 0)


def _dispatch(cnt, pcnt, poff, xn, route, n_rows):
    t = xn.shape[0]
    td = DISPATCH_TILE
    return pl.pallas_call(
        _dispatch_kernel,
        grid_spec=pltpu.PrefetchScalarGridSpec(
            num_scalar_prefetch=3,
            grid=(t // td,),
            in_specs=[pl.BlockSpec((td, D_MODEL), lambda i, *_: (i, 0)),
                      pl.BlockSpec(memory_space=pl.ANY)],
            out_specs=pl.BlockSpec(memory_space=pl.ANY),
            scratch_shapes=[pltpu.SMEM((8, td), I32),
                            pltpu.VMEM((1, D_MODEL), F32),
                            pltpu.SemaphoreType.DMA,
                            pltpu.SemaphoreType.DMA]),
        out_shape=jax.ShapeDtypeStruct((n_rows, D_MODEL), F32),
        compiler_params=_cparams("arbitrary"),
        name="dispatch",
    )(cnt, pcnt, poff, xn, route)


def _expert_kernel(be_ref, nu_ref, xs_ref, w1g_ref, w1l_ref, b1g_ref, b1l_ref, w2_ref, b2_ref,
                   ys_ref):
    i = pl.program_id(0)

    @pl.when(i < nu_ref[0])
    def _():
        x = xs_ref[...].astype(BF16)
        hg = jnp.dot(x, w1g_ref[0], preferred_element_type=F32) + b1g_ref[0]
        hl = jnp.dot(x, w1l_ref[0], preferred_element_type=F32) + b1l_ref[0]
        x_glu = jnp.minimum(hg, SWIGLU_LIMIT)
        x_lin = jnp.clip(hl, -SWIGLU_LIMIT, SWIGLU_LIMIT)
        act = x_glu * jax.nn.sigmoid(SWIGLU_ALPHA * x_glu) * (x_lin + 1.0)
        ys_ref[...] = _dot(act, w2_ref[0]) + b2_ref[0]

    @pl.when(i >= nu_ref[0])
    def _():
        ys_ref[...] = jnp.zeros_like(ys_ref)


def _experts(blk_exp, n_used, xs, w1g, w1l, b1g, b1l, w2, b2):
    n_rows = xs.shape[0]
    tm = MOE_ROWS
    wmap = lambda i, be, nu: (be[i], 0, 0)
    return pl.pallas_call(
        _expert_kernel,
        grid_spec=pltpu.PrefetchScalarGridSpec(
            num_scalar_prefetch=2,
            grid=(n_rows // tm,),
            in_specs=[pl.BlockSpec((tm, D_MODEL), lambda i, be, nu: (jnp.minimum(i, nu[0] - 1), 0)),
                      pl.BlockSpec((1, D_MODEL, D_MODEL), wmap),
                      pl.BlockSpec((1, D_MODEL, D_MODEL), wmap),
                      pl.BlockSpec((1, 1, D_MODEL), wmap),
                      pl.BlockSpec((1, 1, D_MODEL), wmap),
                      pl.BlockSpec((1, D_MODEL, D_MODEL), wmap),
                      pl.BlockSpec((1, 1, D_MODEL), wmap)],
            out_specs=pl.BlockSpec((tm, D_MODEL), lambda i, be, nu: (i, 0))),
        out_shape=jax.ShapeDtypeStruct((n_rows, D_MODEL), F32),
        compiler_params=_cparams("arbitrary"),
        name="experts",
    )(blk_exp, n_used, xs, w1g, w1l, b1g, b1l, w2, b2)


def _combine_kernel(poff_ref, h_ref, gate_ref, nf_ref, route_hbm, ys_hbm, out_ref,
                    route_smem, buf_ref, sem_idx, sem_rows):
    i = pl.program_id(0)
    tc = h_ref.shape[0]

    idx_copy = pltpu.make_async_copy(route_hbm.at[:, pl.ds(i * tc, tc)], route_smem, sem_idx)
    idx_copy.start()
    idx_copy.wait()

    def start(tk, c):
        for kk in range(TOP_K):
            d = poff_ref[route_smem[kk, tk]] + route_smem[TOP_K + kk, tk]
            _row_copy(ys_hbm.at[pl.ds(d, 1)], buf_ref.at[kk, pl.ds(tk, 1)], sem_rows).start()
        return c

    def wait(tk, c):
        for kk in range(TOP_K):
            _row_copy(ys_hbm.at[pl.ds(0, 1)], buf_ref.at[kk, pl.ds(0, 1)], sem_rows).wait()
        return c

    lax.fori_loop(0, tc, start, 0)
    lax.fori_loop(0, tc, wait, 0)

    gate = gate_ref[...]
    acc = jnp.zeros(h_ref.shape, F32)
    for kk in range(TOP_K):
        acc = acc + gate[:, kk:kk + 1] * buf_ref[kk]
    h = h_ref[...] + acc
    out_ref[...] = h * lax.rsqrt(jnp.mean(h * h, axis=-1, keepdims=True) + RMS_EPS) * nf_ref[...]


def _combine(poff, h, gates_tok, norm_f, route, ys):
    t = h.shape[0]
    tc = COMBINE_TILE
    return pl.pallas_call(
        _combine_kernel,
        grid_spec=pltpu.PrefetchScalarGridSpec(
            num_scalar_prefetch=1,
            grid=(t // tc,),
            in_specs=[pl.BlockSpec((tc, D_MODEL), lambda i, *_: (i, 0)),
                      pl.BlockSpec((tc, TOP_K), lambda i, *_: (i, 0)),
                      pl.BlockSpec((1, D_MODEL), lambda i, *_: (0, 0)),
                      pl.BlockSpec(memory_space=pl.ANY),
                      pl.BlockSpec(memory_space=pl.ANY)],
            out_specs=pl.BlockSpec((tc, D_MODEL), lambda i, *_: (i, 0)),
            scratch_shapes=[pltpu.SMEM((8, tc), I32),
                            pltpu.VMEM((TOP_K, tc, D_MODEL), F32),
                            pltpu.SemaphoreType.DMA,
                            pltpu.SemaphoreType.DMA]),
        out_shape=jax.ShapeDtypeStruct((t, D_MODEL), F32),
        compiler_params=_cparams("arbitrary"),
        name="combine",
    )(poff, h, gates_tok, norm_f, route, ys)


def _moe_plan(counts):
    t4 = None
    pcnt = ((counts + MOE_ROWS - 1) // MOE_ROWS) * MOE_ROWS
    pend = jnp.cumsum(pcnt)
    poff = pend - pcnt
    return pcnt.astype(I32), poff.astype(I32), pend.astype(I32)


def kernel(x, norm1_g, w_in, s5_lambda_re, s5_lambda_im, s5_log_dt, s5_b_re, s5_b_im, s5_c_re, s5_c_im, s5_d, s5_w_glu, s5_b_glu, rw_mu, rw_w0, rw_w2, rw_a0, rw_a2, rw_g2, rw_k_k, rw_k_a, rw_r_k, rw_ln_g, rw_ln_b, rw_w_out, w_out, norm2_g, router_w, router_b, moe_w1, moe_b1, moe_w2, moe_b2, norm_f_g):
    bsz, seq, d = x.shape
    t = bsz * seq
    assert d == D_MODEL and norm1_g.shape[0] == 1
    assert seq % IN_TILE == 0 and seq % S5_STEPS == 0 and seq % RW_CHUNK == 0 and bsz % 8 == 0
    assert t % ROUTE_TILE == 0 and t % DISPATCH_TILE == 0 and t % COMBINE_TILE == 0
    x2d = x.reshape(t, d)

    u_tm, prw, gs5_tm, grw = _in_proj(x2d, norm1_g, w_in[0].astype(BF16), bsz, seq)
    a_vec, bd, cd = _s5_params(s5_lambda_re[0], s5_lambda_im[0], s5_log_dt[0], s5_b_re[0],
                               s5_b_im[0], s5_c_re[0], s5_c_im[0])
    s5_tm = _s5_branch(u_tm.reshape(seq * bsz, S5_WIDTH), gs5_tm.reshape(seq * bsz, D_MODEL),
                       a_vec, bd, cd, s5_d, s5_w_glu[0].astype(BF16), s5_b_glu, bsz, seq)
    zpad = jnp.zeros((64, RW_WIDTH), F32)
    w2p = jnp.concatenate([rw_w2[0], zpad], axis=0).astype(BF16)
    a2p = jnp.concatenate([zpad, rw_a2[0]], axis=0).astype(BF16)
    rwg = _rwkv_branch(prw, grw, rw_mu, rw_w0, w2p, rw_a0, a2p, rw_g2[0].astype(BF16), rw_k_k,
                       rw_k_a, rw_r_k.reshape(1, RW_WIDTH), rw_ln_g, rw_ln_b,
                       rw_w_out[0].astype(BF16), bsz, seq)
    h, xn2, logits_t = _mix(x2d, s5_tm.reshape(seq, bsz * D_MODEL), rwg, w_out[0].astype(BF16),
                            norm2_g, router_w[0].T, router_b.reshape(N_EXPERTS, 1), bsz, seq)

    route, gates, cnt2d = _route(logits_t)
    counts = cnt2d[:, 0]
    pcnt, poff, pend = _moe_plan(counts)
    n_blocks = (t * TOP_K) // MOE_ROWS + N_EXPERTS
    n_used = (pend[-1:] // MOE_ROWS).astype(I32)
    blk_exp = jnp.minimum(jnp.searchsorted(pend, jnp.arange(n_blocks, dtype=I32) * MOE_ROWS,
                                           side='right'), N_EXPERTS - 1).astype(I32)
    xs = _dispatch(counts, pcnt, poff, xn2, route, n_blocks * MOE_ROWS)
    w1 = moe_w1[0]
    ys = _experts(blk_exp, n_used, xs,
                  w1[:, :, 0::2].astype(BF16), w1[:, :, 1::2].astype(BF16),
                  moe_b1[0][:, None, 0::2], moe_b1[0][:, None, 1::2],
                  moe_w2[0].astype(BF16), moe_b2[0][:, None, :])
    out = _combine(poff, h, gates[:TOP_K].T, norm_f_g.reshape(1, D_MODEL), route, ys)
    return out.reshape(bsz, seq, d)
```

```python
import functools
import math

import jax
import jax.numpy as jnp
from jax import lax
from jax.experimental import pallas as pl
from jax.experimental.pallas import tpu as pltpu

F32 = jnp.float32
BF16 = jnp.bfloat16
I32 = jnp.int32

D_MODEL = 1024
S5_WIDTH = 512
S5_GROUP = 16
S5_GROUPS = 32
S5_STATE = 64
S5_HALF_GROUPS = 16
S5_HALF_IN = S5_HALF_GROUPS * S5_GROUP
S5_HALF_ST = S5_HALF_GROUPS * S5_STATE
S5_ST_COLS = 4 * S5_HALF_ST
RW_WIDTH = 512
RW_HEAD = 64
RW_HEADS = 8
RW_COLS = 1792
IN_COLS = 4352
N_EXPERTS = 32
TOP_K = 4
SWIGLU_ALPHA = 1.702
SWIGLU_LIMIT = 7.0
RMS_EPS = 1e-5
GN_EPS = 64e-5
L2_EPS = 1e-12

VMEM_LIMIT_BYTES = 48 * 1024 * 1024
EXPERT_VMEM_LIMIT_BYTES = 56 * 1024 * 1024

IN_TILE = 256
S5_STEPS = 32
S5_LANES = 512
RW_CHUNK = 64
ROUTE_TILE = 512
MOE_ROWS = 256
DISPATCH_TILE = 256
COMBINE_TILE = 128


def _cparams(*sem):
    return pltpu.CompilerParams(dimension_semantics=sem, vmem_limit_bytes=VMEM_LIMIT_BYTES)


def _dot(a, b):
    return jnp.dot(a.astype(BF16), b.astype(BF16), preferred_element_type=F32)


def _dot_nt(a, b):
    return lax.dot_general(a.astype(BF16), b.astype(BF16), (((1,), (1,)), ((), ())),
                           preferred_element_type=F32)


def _dot_tn(a, b):
    return lax.dot_general(a.astype(BF16), b.astype(BF16), (((0,), (0,)), ((), ())),
                           preferred_element_type=F32)


def _dot_split(a, b_f32):
    hi = b_f32.astype(BF16)
    lo = (b_f32 - hi.astype(F32)).astype(BF16)
    return (jnp.dot(a, hi, preferred_element_type=F32)
            + jnp.dot(a, lo, preferred_element_type=F32))


def _split_dot(a_f32, b):
    hi = a_f32.astype(BF16)
    lo = (a_f32 - hi.astype(F32)).astype(BF16)
    return (jnp.dot(hi, b, preferred_element_type=F32)
            + jnp.dot(lo, b, preferred_element_type=F32))


def _in_proj_kernel(x_ref, g_ref, w_ref, u_ref, prw_ref, gs5_ref, grw_ref):
    x = x_ref[...]
    xn = x * lax.rsqrt(jnp.mean(x * x, axis=-1, keepdims=True) + RMS_EPS) * g_ref[...]
    p = _dot(xn, w_ref[...])
    c0, c1, c2 = S5_WIDTH, S5_WIDTH + RW_COLS, S5_WIDTH + RW_COLS + D_MODEL
    u_ref[...] = p[:, :c0].astype(BF16)
    prw_ref[...] = p[:, c0:c1]
    gs5_ref[...] = jax.nn.sigmoid(p[:, c1:c2]).astype(BF16)
    grw_ref[...] = jax.nn.sigmoid(p[:, c2:]).astype(BF16)


def _in_proj(x2d, norm_g, w_in_bf, bsz, seq):
    ts = IN_TILE
    t = bsz * seq
    row = lambda i: (i, 0)
    return pl.pallas_call(
        _in_proj_kernel,
        grid=(t // ts,),
        in_specs=[pl.BlockSpec((ts, D_MODEL), row),
                  pl.BlockSpec((1, D_MODEL), lambda i: (0, 0)),
                  pl.BlockSpec((D_MODEL, IN_COLS), lambda i: (0, 0))],
        out_specs=[pl.BlockSpec((ts, S5_WIDTH), row),
                   pl.BlockSpec((ts, RW_COLS), row),
                   pl.BlockSpec((ts, D_MODEL), row),
                   pl.BlockSpec((ts, D_MODEL), row)],
        out_shape=[jax.ShapeDtypeStruct((t, S5_WIDTH), BF16),
                   jax.ShapeDtypeStruct((t, RW_COLS), F32),
                   jax.ShapeDtypeStruct((t, D_MODEL), BF16),
                   jax.ShapeDtypeStruct((t, D_MODEL), BF16)],
        compiler_params=_cparams("parallel"),
        name="in_proj",
    )(x2d, norm_g, w_in_bf)


def _s5_kernel(bsz, u_ref, gs5_ref, perm_ref, a_ref, bd_ref, cd_ref, d_ref, wg_ref, bg_ref,
               out_ref, st_ref, xb_ref):
    @pl.when(pl.program_id(0) == 0)
    def _():
        st_ref[...] = jnp.zeros_like(st_ref)

    rows = bsz * S5_STEPS
    u = jnp.dot(perm_ref[0], u_ref[...].reshape(rows, S5_WIDTH),
                preferred_element_type=F32).astype(BF16)
    for h in range(2):
        xb_ref[:, 2 * S5_HALF_ST * h:2 * S5_HALF_ST * (h + 1)] = jnp.dot(
            u[:, S5_HALF_IN * h:S5_HALF_IN * (h + 1)], bd_ref[h], preferred_element_type=F32)

    for h in range(2):
        for j in range(S5_HALF_ST // S5_LANES):
            ro = 2 * S5_HALF_ST * h + S5_LANES * j
            io = ro + S5_HALF_ST
            ar = jnp.broadcast_to(a_ref[:, ro:ro + S5_LANES], (bsz, S5_LANES))
            ai = jnp.broadcast_to(a_ref[:, io:io + S5_LANES], (bsz, S5_LANES))

            def step(t, carry, ro=ro, io=io, ar=ar, ai=ai):
                xr, xi = carry
                r0 = pl.multiple_of(t * bsz, bsz)
                nr = ar * xr - ai * xi + xb_ref[pl.ds(r0, bsz), ro:ro + S5_LANES]
                ni = ar * xi + ai * xr + xb_ref[pl.ds(r0, bsz), io:io + S5_LANES]
                xb_ref[pl.ds(r0, bsz), ro:ro + S5_LANES] = nr
                xb_ref[pl.ds(r0, bsz), io:io + S5_LANES] = ni
                return nr, ni

            xr, xi = lax.fori_loop(0, S5_STEPS, step,
                                   (st_ref[:, ro:ro + S5_LANES], st_ref[:, io:io + S5_LANES]))
            st_ref[:, ro:ro + S5_LANES] = xr
            st_ref[:, io:io + S5_LANES] = xi

    ys = []
    for h in range(2):
        xh = xb_ref[:, 2 * S5_HALF_ST * h:2 * S5_HALF_ST * (h + 1)]
        ys.append(jnp.dot(xh.astype(BF16), cd_ref[h], preferred_element_type=F32))
    y = jnp.concatenate(ys, axis=1) + d_ref[...] * u.astype(F32)
    y = jax.nn.gelu(y)
    z = _dot(y, wg_ref[...]) + bg_ref[...]
    s5 = (z[:, :D_MODEL] * jax.nn.sigmoid(z[:, D_MODEL:])).astype(BF16)
    s5 = jnp.dot(perm_ref[1], s5, preferred_element_type=F32)
    gate = gs5_ref[...].reshape(rows, D_MODEL).astype(F32)
    out_ref[...] = (s5 * gate).astype(BF16).reshape(bsz, S5_STEPS, D_MODEL)


def _s5_params(lam_re, lam_im, log_dt, b_re, b_im, c_re, c_im):
    dt = jnp.exp(log_dt)[:, None]
    mag = jnp.exp(lam_re * dt)
    ab_re, ab_im = mag * jnp.cos(lam_im * dt), mag * jnp.sin(lam_im * dt)
    den = lam_re * lam_re + lam_im * lam_im
    nr, ni = ab_re - 1.0, ab_im
    f_re = (nr * lam_re + ni * lam_im) / den
    f_im = (ni * lam_re - nr * lam_im) / den
    bb_re = f_re[..., None] * b_re - f_im[..., None] * b_im
    bb_im = f_re[..., None] * b_im + f_im[..., None] * b_re
    eye = jnp.eye(S5_HALF_GROUPS, dtype=F32)

    def bd_half(bb):
        return jnp.einsum('gpc,gh->gchp', bb, eye).reshape(S5_HALF_IN, S5_HALF_ST)

    def cd_half(cc):
        return jnp.einsum('gcp,gh->gphc', cc, eye).reshape(S5_HALF_ST, S5_HALF_IN)

    bd, cd, a = [], [], []
    for h in range(2):
        sl = slice(S5_HALF_GROUPS * h, S5_HALF_GROUPS * (h + 1))
        bd.append(jnp.concatenate([bd_half(bb_re[sl]), bd_half(bb_im[sl])], axis=1))
        cd.append(jnp.concatenate([cd_half(c_re[sl]), cd_half(-c_im[sl])], axis=0))
        a += [ab_re[sl].reshape(1, S5_HALF_ST), ab_im[sl].reshape(1, S5_HALF_ST)]
    return (jnp.concatenate(a, axis=1), jnp.stack(bd).astype(BF16), jnp.stack(cd).astype(BF16))


def _s5_branch(u, gs5, a_vec, bd, cd, d_skip, w_glu_bf, b_glu, bsz, seq):
    rows = S5_STEPS * bsz
    const2 = lambda i: (0, 0)
    const3 = lambda i: (0, 0, 0)
    r_tb = jnp.arange(rows, dtype=I32)
    r_bt = (r_tb % bsz) * S5_STEPS + r_tb // bsz
    to_tb = (r_bt[:, None] == jnp.arange(rows, dtype=I32)[None, :]).astype(BF16)
    perm = jnp.stack([to_tb, to_tb.T])
    blk = lambda i: (0, i, 0)
    return pl.pallas_call(
        functools.partial(_s5_kernel, bsz),
        grid=(seq // S5_STEPS,),
        in_specs=[pl.BlockSpec((bsz, S5_STEPS, S5_WIDTH), blk),
                  pl.BlockSpec((bsz, S5_STEPS, D_MODEL), blk),
                  pl.BlockSpec((2, rows, rows), const3),
                  pl.BlockSpec((1, S5_ST_COLS), const2),
                  pl.BlockSpec((2, S5_HALF_IN, 2 * S5_HALF_ST), const3),
                  pl.BlockSpec((2, 2 * S5_HALF_ST, S5_HALF_IN), const3),
                  pl.BlockSpec((1, S5_WIDTH), const2),
                  pl.BlockSpec((S5_WIDTH, 2 * D_MODEL), const2),
                  pl.BlockSpec((1, 2 * D_MODEL), const2)],
        out_specs=pl.BlockSpec((bsz, S5_STEPS, D_MODEL), blk),
        out_shape=jax.ShapeDtypeStruct((bsz, seq, D_MODEL), BF16),
        scratch_shapes=[pltpu.VMEM((bsz, S5_ST_COLS), F32),
                        pltpu.VMEM((rows, S5_ST_COLS), F32)],
        compiler_params=_cparams("arbitrary"),
        name="s5_branch",
    )(u, gs5, perm, a_vec, bd, cd, d_skip, w_glu_bf, b_glu)


def _rwkv_kernel(p_ref, grw_ref, mu_ref, w0_ref, w2_ref, a0_ref, a2_ref, g2_ref, kk_ref,
                 ka_ref, rk_ref, lng_ref, lnb_ref, wo_ref, hsum_ref, tri_ref,
                 out_ref, st_ref, prev_ref):
    L, N, H = RW_CHUNK, RW_HEAD, RW_HEADS

    @pl.when(pl.program_id(1) == 0)
    def _():
        st_ref[...] = jnp.zeros_like(st_ref)
        prev_ref[...] = jnp.zeros_like(prev_ref)

    p = p_ref[...]
    first = lax.broadcasted_iota(I32, (L, 1), 0) == 0
    prev = jnp.where(first, prev_ref[...], pltpu.roll(p, 1, axis=0))
    prev_ref[...] = p[L - 1:L, :]
    xs = p + mu_ref[...] * (prev - p)
    c1, c2, c3 = RW_WIDTH, 2 * RW_WIDTH, 3 * RW_WIDTH
    r, k, v = xs[:, :c1], xs[:, c1:c2], xs[:, c2:c3]
    lora = xs[:, c3:c3 + 128]
    gd = xs[:, c3 + 128:]
    hsum = hsum_ref[...]

    w_log = -jax.nn.softplus(-(w0_ref[...] + _dot(jnp.tanh(lora), w2_ref[...]))) - 0.5
    e = jnp.exp(w_log)
    a = jax.nn.sigmoid(a0_ref[...] + _dot(lora, a2_ref[...]))
    g = _dot(jax.nn.sigmoid(gd), g2_ref[...])
    kk = k * kk_ref[...]
    kk = kk / jnp.maximum(jnp.sqrt(_split_dot(kk * kk, hsum)), L2_EPS)
    k = k * (1.0 + (a - 1.0) * ka_ref[...])
    avec, bvec = -kk, kk * a

    cum = _dot_split(tri_ref[...], e)
    cum_last = cum[L - 1:L, :]
    dec_in = jnp.exp(-cum)
    at = avec * jnp.exp(e - cum)
    rt = r * dec_in
    grow = jnp.exp(cum)
    bt, kt = bvec * grow, k * grow
    tail = jnp.exp(cum - cum_last)
    bw, kw = bvec * tail, k * tail
    w_all = jnp.exp(-cum_last)

    ti = lax.broadcasted_iota(I32, (L, 2 * L), 0)
    si = lax.broadcasted_iota(I32, (L, 2 * L), 1)
    m_ab = jnp.where((si < L) & (si < ti), 1.0, 0.0)
    m_ak = jnp.where((si >= L) & (si - L < ti), 1.0, 0.0)
    m_rr = jnp.where(jnp.where(si >= L, si - L, si) <= ti, 1.0, 0.0)

    heads = range(H)
    hs = [slice(N * h, N * (h + 1)) for h in heads]
    vh = [v[:, s] for s in hs]
    s_old = [st_ref[h] for h in heads]
    ar_ = [jnp.concatenate([at[:, s], rt[:, s]], axis=0) for s in hs]
    bk_ = [jnp.concatenate([bt[:, s], kt[:, s]], axis=0) for s in hs]
    pm = [_dot_nt(ar_[h], bk_[h]) for h in heads]
    ars = [_dot_nt(ar_[h], s_old[h]) for h in heads]
    n_pow = [(pm[h][:L] * m_ab)[:, :L] for h in heads]
    x = [ars[h][:L] + _dot(pm[h][:L] * m_ak, jnp.concatenate([vh[h], vh[h]], axis=0))
         for h in heads]
    for i in range(6):
        x = [x[h] + _dot(n_pow[h], x[h]) for h in heads]
        if i < 5:
            n_pow = [_dot(n_pow[h], n_pow[h]) for h in heads]
    uv = [jnp.concatenate([x[h], vh[h]], axis=0) for h in heads]
    ys = [ars[h][L:] + _dot(pm[h][L:] * m_rr, uv[h]) for h in heads]
    for h in heads:
        bkw = jnp.concatenate([bw[:, hs[h]], kw[:, hs[h]]], axis=0)
        st_ref[h] = s_old[h] * w_all[:, hs[h]] + _dot_tn(uv[h], bkw)
    y = jnp.concatenate(ys, axis=1)

    inv_n = 1.0 / N
    mean = _split_dot(y, hsum) * inv_n
    dlt = y - mean
    var = _split_dot(dlt * dlt, hsum) * inv_n
    y = dlt * lax.rsqrt(var + GN_EPS) * lng_ref[...] + lnb_ref[...]
    bonus = _split_dot(r * k * rk_ref[...], hsum) * v
    y = (y + bonus) * g
    out_ref[...] = (_dot(y, wo_ref[...]) * grw_ref[...].astype(F32)).astype(BF16)


def _rwkv_branch(prw, grw, mu, w0, w2p, a0, a2p, g2, k_k, k_a, r_k, ln_g, ln_b, wo_bf, bsz, seq):
    L = RW_CHUNK
    nc = seq // L
    row = lambda b, c: (b * nc + c, 0)
    const = lambda b, c: (0, 0)
    hsum = jnp.kron(jnp.eye(RW_HEADS, dtype=F32), jnp.ones((RW_HEAD, RW_HEAD), F32)).astype(BF16)
    tri = jnp.tril(jnp.ones((L, L), F32)).astype(BF16)
    vec = lambda n: pl.BlockSpec((1, n), const)
    return pl.pallas_call(
        _rwkv_kernel,
        grid=(bsz, nc),
        in_specs=[pl.BlockSpec((L, RW_COLS), row),
                  pl.BlockSpec((L, D_MODEL), row),
                  vec(RW_COLS), vec(RW_WIDTH),
                  pl.BlockSpec((128, RW_WIDTH), const),
                  vec(RW_WIDTH),
                  pl.BlockSpec((128, RW_WIDTH), const),
                  pl.BlockSpec((128, RW_WIDTH), const),
                  vec(RW_WIDTH), vec(RW_WIDTH), vec(RW_WIDTH), vec(RW_WIDTH), vec(RW_WIDTH),
                  pl.BlockSpec((RW_WIDTH, D_MODEL), const),
                  pl.BlockSpec((RW_WIDTH, RW_WIDTH), const),
                  pl.BlockSpec((L, L), const)],
        out_specs=pl.BlockSpec((L, D_MODEL), row),
        out_shape=jax.ShapeDtypeStruct((bsz * seq, D_MODEL), BF16),
        scratch_shapes=[pltpu.VMEM((RW_HEADS, RW_HEAD, RW_HEAD), F32),
                        pltpu.VMEM((1, RW_COLS), F32)],
        compiler_params=_cparams("arbitrary", "arbitrary"),
        name="rwkv_branch",
    )(prw, grw, mu, w0, w2p, a0, a2p, g2, k_k, k_a, r_k, ln_g, ln_b, wo_bf, hsum, tri)


def _mix_kernel(x_ref, s5_ref, rw_ref, wout_ref, g2_ref, rwt_ref, rb_ref,
                h_ref, xn_ref, lg_ref):
    mixed = s5_ref[...].astype(F32) + rw_ref[...].astype(F32)
    h = x_ref[...] + _dot(mixed, wout_ref[...])
    h_ref[...] = h
    xn = h * lax.rsqrt(jnp.mean(h * h, axis=-1, keepdims=True) + RMS_EPS) * g2_ref[...]
    xn_ref[...] = xn
    lg_ref[...] = lax.dot_general(rwt_ref[...], xn, (((1,), (1,)), ((), ())),
                                  precision=lax.Precision.HIGHEST,
                                  preferred_element_type=F32) + rb_ref[...]


def _mix(x2d, s5g, rwg, w_out_bf, norm2_g, router_wt, router_b):
    ts = IN_TILE
    t = x2d.shape[0]
    row = lambda i: (i, 0)
    const = lambda i: (0, 0)
    return pl.pallas_call(
        _mix_kernel,
        grid=(t // ts,),
        in_specs=[pl.BlockSpec((ts, D_MODEL), row),
                  pl.BlockSpec((ts, D_MODEL), row),
                  pl.BlockSpec((ts, D_MODEL), row),
                  pl.BlockSpec((D_MODEL, D_MODEL), const),
                  pl.BlockSpec((1, D_MODEL), const),
                  pl.BlockSpec((N_EXPERTS, D_MODEL), const),
                  pl.BlockSpec((N_EXPERTS, 1), const)],
        out_specs=[pl.BlockSpec((ts, D_MODEL), row),
                   pl.BlockSpec((ts, D_MODEL), row),
                   pl.BlockSpec((N_EXPERTS, ts), lambda i: (0, i))],
        out_shape=[jax.ShapeDtypeStruct((t, D_MODEL), F32),
                   jax.ShapeDtypeStruct((t, D_MODEL), F32),
                   jax.ShapeDtypeStruct((N_EXPERTS, t), F32)],
        compiler_params=_cparams("parallel"),
        name="mix_out_proj",
    )(x2d, s5g, rwg, w_out_bf, norm2_g, router_wt, router_b)


def _route_kernel(lg_ref, triu_ref, route_ref, gate_ref, cnt_ref, run_ref):
    @pl.when(pl.program_id(0) == 0)
    def _():
        run_ref[...] = jnp.zeros_like(run_ref)

    lg = lg_ref[...]
    tr = lg.shape[1]
    ie = lax.broadcasted_iota(I32, lg.shape, 0)
    sel = jnp.zeros(lg.shape, F32)
    hots, vals, idxs = [], [], []
    for _ in range(TOP_K):
        m = jnp.max(lg, axis=0, keepdims=True)
        idx = jnp.min(jnp.where(lg == m, ie, N_EXPERTS), axis=0, keepdims=True)
        hot = ie == idx
        hots.append(hot)
        vals.append(m)
        idxs.append(idx)
        sel = jnp.where(hot, 1.0, sel)
        lg = jnp.where(hot, -jnp.inf, lg)
    ex = [jnp.exp(vk - vals[0]) for vk in vals]
    den = ex[0] + ex[1] + ex[2] + ex[3]
    before = jnp.dot(sel.astype(BF16), triu_ref[...], preferred_element_type=F32) + run_ref[:, 0:1]
    ranks = [jnp.sum(jnp.where(hot, before, 0.0), axis=0, keepdims=True) for hot in hots]
    run = run_ref[...] + jnp.sum(sel, axis=1, keepdims=True)
    run_ref[...] = run
    cnt_ref[...] = run.astype(I32)
    route_ref[...] = jnp.concatenate(idxs + [rk.astype(I32) for rk in ranks], axis=0)
    gate_ref[...] = jnp.concatenate([ek / den for ek in ex] + [jnp.zeros((4, tr), F32)], axis=0)


def _route(logits_t):
    t = logits_t.shape[1]
    tr = ROUTE_TILE
    triu = jnp.triu(jnp.ones((tr, tr), F32), k=1).astype(BF16)
    return pl.pallas_call(
        _route_kernel,
        grid=(t // tr,),
        in_specs=[pl.BlockSpec((N_EXPERTS, tr), lambda i: (0, i)),
                  pl.BlockSpec((tr, tr), lambda i: (0, 0))],
        out_specs=[pl.BlockSpec((8, tr), lambda i: (0, i)),
                   pl.BlockSpec((8, tr), lambda i: (0, i)),
                   pl.BlockSpec((N_EXPERTS, 128), lambda i: (0, 0))],
        out_shape=[jax.ShapeDtypeStruct((8, t), I32),
                   jax.ShapeDtypeStruct((8, t), F32),
                   jax.ShapeDtypeStruct((N_EXPERTS, 128), I32)],
        scratch_shapes=[pltpu.VMEM((N_EXPERTS, 128), F32)],
        compiler_params=_cparams("arbitrary"),
        name="route",
    )(logits_t, triu)


def _row_copy(src, dst, sem):
    return pltpu.make_async_copy(src, dst, sem)


def _dispatch_kernel(cnt_ref, pcnt_ref, poff_ref, xn_ref, route_hbm, xs_hbm,
                     route_smem, zero_ref, sem_idx, sem_rows):
    i = pl.program_id(0)
    td = xn_ref.shape[0]

    idx_copy = pltpu.make_async_copy(route_hbm.at[:, pl.ds(i * td, td)], route_smem, sem_idx)
    idx_copy.start()

    @pl.when(i == 0)
    def _():
        zero_ref[...] = jnp.zeros_like(zero_ref)

        def per_expert(e, carry):
            base = poff_ref[e]

            def start(rw, c):
                _row_copy(zero_ref, xs_hbm.at[pl.ds(base + rw, 1)], sem_rows).start()
                return c

            def wait(rw, c):
                _row_copy(zero_ref, xs_hbm.at[pl.ds(base + rw, 1)], sem_rows).wait()
                return c

            lax.fori_loop(cnt_ref[e], pcnt_ref[e], start, 0)
            lax.fori_loop(cnt_ref[e], pcnt_ref[e], wait, 0)
            return carry

        lax.fori_loop(0, N_EXPERTS, per_expert, 0)

    idx_copy.wait()

    def start(tk, c):
        for kk in range(TOP_K):
            d = poff_ref[route_smem[kk, tk]] + route_smem[TOP_K + kk, tk]
            _row_copy(xn_ref.at[pl.ds(tk, 1)], xs_hbm.at[pl.ds(d, 1)], sem_rows).start()
        return c

    lax.fori_loop(0, td, start, 0, unroll=8)
    for _ in range(TOP_K):
        _row_copy(xn_ref, xs_hbm.at[pl.ds(0, td)], sem_rows).wait()


def _dispatch(cnt, pcnt, poff, xn, route, n_rows):
    t = xn.shape[0]
    td = DISPATCH_TILE
    return pl.pallas_call(
        _dispatch_kernel,
        grid_spec=pltpu.PrefetchScalarGridSpec(
            num_scalar_prefetch=3,
            grid=(t // td,),
            in_specs=[pl.BlockSpec((td, D_MODEL), lambda i, *_: (i, 0)),
                      pl.BlockSpec(memory_space=pl.ANY)],
            out_specs=pl.BlockSpec(memory_space=pl.ANY),
            scratch_shapes=[pltpu.SMEM((8, td), I32),
                            pltpu.VMEM((1, D_MODEL), F32),
                            pltpu.SemaphoreType.DMA,
                            pltpu.SemaphoreType.DMA]),
        out_shape=jax.ShapeDtypeStruct((n_rows, D_MODEL), F32),
        compiler_params=_cparams("arbitrary"),
        name="dispatch",
    )(cnt, pcnt, poff, xn, route)


DEINT = 256


def _expert_kernel(be_ref, nu_ref, xs_ref, w1_ref, b1g_ref, b1l_ref, w2_ref, b2_ref, sel_ref,
                   ys_ref, w1g_ref, w1l_ref, w2b_ref):
    i = pl.program_id(0)
    new_expert = jnp.logical_or(i == 0, be_ref[i] != be_ref[jnp.maximum(i - 1, 0)])

    @pl.when(jnp.logical_and(new_expert, i < nu_ref[0]))
    def _():
        half = DEINT // 2
        for c in range(2 * D_MODEL // DEINT):
            cols = w1_ref[0, :, DEINT * c:DEINT * (c + 1)].astype(BF16)
            sep = jnp.dot(cols, sel_ref[...], preferred_element_type=F32)
            w1g_ref[:, half * c:half * (c + 1)] = sep[:, :half].astype(BF16)
            w1l_ref[:, half * c:half * (c + 1)] = sep[:, half:].astype(BF16)
        w2b_ref[...] = w2_ref[0].astype(BF16)

    @pl.when(i < nu_ref[0])
    def _():
        x = xs_ref[...].astype(BF16)
        hg = jnp.dot(x, w1g_ref[...], preferred_element_type=F32) + b1g_ref[0]
        hl = jnp.dot(x, w1l_ref[...], preferred_element_type=F32) + b1l_ref[0]
        x_glu = jnp.minimum(hg, SWIGLU_LIMIT)
        x_lin = jnp.clip(hl, -SWIGLU_LIMIT, SWIGLU_LIMIT)
        act = x_glu * jax.nn.sigmoid(SWIGLU_ALPHA * x_glu) * (x_lin + 1.0)
        ys_ref[...] = jnp.dot(act.astype(BF16), w2b_ref[...],
                              preferred_element_type=F32) + b2_ref[0]

    @pl.when(i >= nu_ref[0])
    def _():
        ys_ref[...] = jnp.zeros_like(ys_ref)


def _experts(blk_exp, n_used, xs, w1, b1g, b1l, w2, b2):
    n_rows = xs.shape[0]
    tm = MOE_ROWS
    wmap = lambda i, be, nu: (be[i], 0, 0)
    j = jnp.arange(DEINT, dtype=I32)
    sel = (jnp.where(j % 2 == 0, j // 2, DEINT // 2 + j // 2)[:, None] == j[None, :]).astype(BF16)
    return pl.pallas_call(
        _expert_kernel,
        grid_spec=pltpu.PrefetchScalarGridSpec(
            num_scalar_prefetch=2,
            grid=(n_rows // tm,),
            in_specs=[pl.BlockSpec((tm, D_MODEL), lambda i, be, nu: (jnp.minimum(i, nu[0] - 1), 0)),
                      pl.BlockSpec((1, D_MODEL, 2 * D_MODEL), wmap),
                      pl.BlockSpec((1, 1, D_MODEL), wmap),
                      pl.BlockSpec((1, 1, D_MODEL), wmap),
                      pl.BlockSpec((1, D_MODEL, D_MODEL), wmap),
                      pl.BlockSpec((1, 1, D_MODEL), wmap),
                      pl.BlockSpec((DEINT, DEINT), lambda i, be, nu: (0, 0))],
            out_specs=pl.BlockSpec((tm, D_MODEL), lambda i, be, nu: (i, 0)),
            scratch_shapes=[pltpu.VMEM((D_MODEL, D_MODEL), BF16),
                            pltpu.VMEM((D_MODEL, D_MODEL), BF16),
                            pltpu.VMEM((D_MODEL, D_MODEL), BF16)]),
        out_shape=jax.ShapeDtypeStruct((n_rows, D_MODEL), F32),
        compiler_params=pltpu.CompilerParams(dimension_semantics=("arbitrary",),
                                             vmem_limit_bytes=EXPERT_VMEM_LIMIT_BYTES),
        name="experts",
    )(blk_exp, n_used, xs, w1, b1g, b1l, w2, b2, sel)


def _combine_kernel(poff_ref, h_ref, gate_ref, nf_ref, route_hbm, ys_hbm, out_ref,
                    route_smem, buf_ref, sem_idx, sem_rows):
    i = pl.program_id(0)
    tc = h_ref.shape[0]

    idx_copy = pltpu.make_async_copy(route_hbm.at[:, pl.ds(i * tc, tc)], route_smem, sem_idx)
    idx_copy.start()
    idx_copy.wait()

    def start(tk, c):
        for kk in range(TOP_K):
            d = poff_ref[route_smem[kk, tk]] + route_smem[TOP_K + kk, tk]
            _row_copy(ys_hbm.at[pl.ds(d, 1)], buf_ref.at[kk, pl.ds(tk, 1)], sem_rows).start()
        return c

    lax.fori_loop(0, tc, start, 0, unroll=8)
    for kk in range(TOP_K):
        _row_copy(ys_hbm.at[pl.ds(0, tc)], buf_ref.at[kk], sem_rows).wait()

    gate = gate_ref[...]
    acc = jnp.zeros(h_ref.shape, F32)
    for kk in range(TOP_K):
        acc = acc + gate[:, kk:kk + 1] * buf_ref[kk]
    h = h_ref[...] + acc
    out_ref[...] = h * lax.rsqrt(jnp.mean(h * h, axis=-1, keepdims=True) + RMS_EPS) * nf_ref[...]


def _combine(poff, h, gates_tok, norm_f, route, ys):
    t = h.shape[0]
    tc = COMBINE_TILE
    return pl.pallas_call(
        _combine_kernel,
        grid_spec=pltpu.PrefetchScalarGridSpec(
            num_scalar_prefetch=1,
            grid=(t // tc,),
            in_specs=[pl.BlockSpec((tc, D_MODEL), lambda i, *_: (i, 0)),
                      pl.BlockSpec((tc, TOP_K), lambda i, *_: (i, 0)),
                      pl.BlockSpec((1, D_MODEL), lambda i, *_: (0, 0)),
                      pl.BlockSpec(memory_space=pl.ANY),
                      pl.BlockSpec(memory_space=pl.ANY)],
            out_specs=pl.BlockSpec((tc, D_MODEL), lambda i, *_: (i, 0)),
            scratch_shapes=[pltpu.SMEM((8, tc), I32),
                            pltpu.VMEM((TOP_K, tc, D_MODEL), F32),
                            pltpu.SemaphoreType.DMA,
                            pltpu.SemaphoreType.DMA]),
        out_shape=jax.ShapeDtypeStruct((t, D_MODEL), F32),
        compiler_params=_cparams("arbitrary"),
        name="combine",
    )(poff, h, gates_tok, norm_f, route, ys)


def _moe_plan(counts, n_blocks):
    pcnt = ((counts + MOE_ROWS - 1) // MOE_ROWS) * MOE_ROWS
    pend = jnp.cumsum(pcnt)
    poff = pend - pcnt
    n_used = (pend[-1:] // MOE_ROWS).astype(I32)
    blk_exp = jnp.minimum(jnp.searchsorted(pend, jnp.arange(n_blocks, dtype=I32) * MOE_ROWS,
                                           side='right'), N_EXPERTS - 1).astype(I32)
    return pcnt.astype(I32), poff.astype(I32), n_used, blk_exp


def kernel(x, norm1_g, w_in, s5_lambda_re, s5_lambda_im, s5_log_dt, s5_b_re, s5_b_im, s5_c_re, s5_c_im, s5_d, s5_w_glu, s5_b_glu, rw_mu, rw_w0, rw_w2, rw_a0, rw_a2, rw_g2, rw_k_k, rw_k_a, rw_r_k, rw_ln_g, rw_ln_b, rw_w_out, w_out, norm2_g, router_w, router_b, moe_w1, moe_b1, moe_w2, moe_b2, norm_f_g):
    bsz, seq, d = x.shape
    t = bsz * seq
    assert d == D_MODEL and norm1_g.shape[0] == 1
    assert seq % S5_STEPS == 0 and seq % RW_CHUNK == 0 and bsz % 8 == 0
    assert t % IN_TILE == 0 and t % ROUTE_TILE == 0 and t % DISPATCH_TILE == 0
    assert t % COMBINE_TILE == 0
    x2d = x.reshape(t, d)

    u, prw, gs5, grw = _in_proj(x2d, norm1_g, w_in[0].astype(BF16), bsz, seq)
    a_vec, bd, cd = _s5_params(s5_lambda_re[0], s5_lambda_im[0], s5_log_dt[0], s5_b_re[0],
                               s5_b_im[0], s5_c_re[0], s5_c_im[0])
    s5g = _s5_branch(u.reshape(bsz, seq, S5_WIDTH), gs5.reshape(bsz, seq, D_MODEL),
                     a_vec, bd, cd, s5_d, s5_w_glu[0].astype(BF16), s5_b_glu, bsz, seq)
    zpad = jnp.zeros((64, RW_WIDTH), F32)
    w2p = jnp.concatenate([rw_w2[0], zpad], axis=0).astype(BF16)
    a2p = jnp.concatenate([zpad, rw_a2[0]], axis=0).astype(BF16)
    rwg = _rwkv_branch(prw, grw, rw_mu, rw_w0, w2p, rw_a0, a2p, rw_g2[0].astype(BF16), rw_k_k,
                       rw_k_a, rw_r_k.reshape(1, RW_WIDTH), rw_ln_g, rw_ln_b,
                       rw_w_out[0].astype(BF16), bsz, seq)
    h, xn2, logits_t = _mix(x2d, s5g.reshape(t, D_MODEL), rwg, w_out[0].astype(BF16),
                            norm2_g, router_w[0].T, router_b.reshape(N_EXPERTS, 1))

    route, gates, cnt2d = _route(logits_t)
    counts = cnt2d[:, 0]
    n_blocks = (t * TOP_K) // MOE_ROWS + N_EXPERTS
    pcnt, poff, n_used, blk_exp = _moe_plan(counts, n_blocks)
    xs = _dispatch(counts, pcnt, poff, xn2, route, n_blocks * MOE_ROWS)
    ys = _experts(blk_exp, n_used, xs, moe_w1[0],
                  moe_b1[0][:, None, 0::2], moe_b1[0][:, None, 1::2],
                  moe_w2[0], moe_b2[0][:, None, :])
    out = _combine(poff, h, gates[:TOP_K].T, norm_f_g.reshape(1, D_MODEL), route, ys)
    return out.reshape(bsz, seq, d)
```

```python
import functools
import math

import jax
import jax.numpy as jnp
from jax import lax
from jax.experimental import pallas as pl
from jax.experimental.pallas import tpu as pltpu

F32 = jnp.float32
BF16 = jnp.bfloat16
I32 = jnp.int32

D_MODEL = 1024
S5_WIDTH = 512
S5_GROUP = 16
S5_GROUPS = 32
S5_STATE = 64
S5_HALF_GROUPS = 16
S5_HALF_IN = S5_HALF_GROUPS * S5_GROUP
S5_HALF_ST = S5_HALF_GROUPS * S5_STATE
S5_ST_COLS = 4 * S5_HALF_ST
RW_WIDTH = 512
RW_HEAD = 64
RW_HEADS = 8
RW_COLS = 1792
IN_COLS = 4352
N_EXPERTS = 32
TOP_K = 4
SWIGLU_ALPHA = 1.702
SWIGLU_LIMIT = 7.0
RMS_EPS = 1e-5
GN_EPS = 64e-5
L2_EPS = 1e-12

VMEM_LIMIT_BYTES = 48 * 1024 * 1024
EXPERT_VMEM_LIMIT_BYTES = 56 * 1024 * 1024

IN_TILE = 256
MIX_TILE = 512
S5_STEPS = 32
S5_LANES = 512
RW_CHUNK = 64
ROUTE_TILE = 512
MOE_ROWS = 512
DISPATCH_TILE = 256
COMBINE_TILE = 128


def _cparams(*sem):
    return pltpu.CompilerParams(dimension_semantics=sem, vmem_limit_bytes=VMEM_LIMIT_BYTES)


def _dot(a, b):
    return jnp.dot(a.astype(BF16), b.astype(BF16), preferred_element_type=F32)


def _dot_nt(a, b):
    return lax.dot_general(a.astype(BF16), b.astype(BF16), (((1,), (1,)), ((), ())),
                           preferred_element_type=F32)


def _dot_tn(a, b):
    return lax.dot_general(a.astype(BF16), b.astype(BF16), (((0,), (0,)), ((), ())),
                           preferred_element_type=F32)


def _dot_split(a, b_f32):
    hi = b_f32.astype(BF16)
    lo = (b_f32 - hi.astype(F32)).astype(BF16)
    return (jnp.dot(a, hi, preferred_element_type=F32)
            + jnp.dot(a, lo, preferred_element_type=F32))


def _split_dot(a_f32, b):
    hi = a_f32.astype(BF16)
    lo = (a_f32 - hi.astype(F32)).astype(BF16)
    return (jnp.dot(hi, b, preferred_element_type=F32)
            + jnp.dot(lo, b, preferred_element_type=F32))


def _in_proj_kernel(x_ref, g_ref, w_ref, u_ref, prw_ref, gs5_ref, grw_ref):
    x = x_ref[...]
    xn = x * lax.rsqrt(jnp.mean(x * x, axis=-1, keepdims=True) + RMS_EPS) * g_ref[...]
    p = _dot(xn, w_ref[...])
    c0, c1, c2 = S5_WIDTH, S5_WIDTH + RW_COLS, S5_WIDTH + RW_COLS + D_MODEL
    u_ref[...] = p[:, :c0].astype(BF16)
    prw_ref[...] = p[:, c0:c1]
    gs5_ref[...] = jax.nn.sigmoid(p[:, c1:c2]).astype(BF16)
    grw_ref[...] = jax.nn.sigmoid(p[:, c2:]).astype(BF16)


def _in_proj(x2d, norm_g, w_in_bf, bsz, seq):
    ts = IN_TILE
    t = bsz * seq
    row = lambda i: (i, 0)
    return pl.pallas_call(
        _in_proj_kernel,
        grid=(t // ts,),
        in_specs=[pl.BlockSpec((ts, D_MODEL), row),
                  pl.BlockSpec((1, D_MODEL), lambda i: (0, 0)),
                  pl.BlockSpec((D_MODEL, IN_COLS), lambda i: (0, 0))],
        out_specs=[pl.BlockSpec((ts, S5_WIDTH), row),
                   pl.BlockSpec((ts, RW_COLS), row),
                   pl.BlockSpec((ts, D_MODEL), row),
                   pl.BlockSpec((ts, D_MODEL), row)],
        out_shape=[jax.ShapeDtypeStruct((t, S5_WIDTH), BF16),
                   jax.ShapeDtypeStruct((t, RW_COLS), F32),
                   jax.ShapeDtypeStruct((t, D_MODEL), BF16),
                   jax.ShapeDtypeStruct((t, D_MODEL), BF16)],
        compiler_params=_cparams("parallel"),
        name="in_proj",
    )(x2d, norm_g, w_in_bf)


def _s5_kernel(bsz, u_ref, gs5_ref, perm_ref, a_ref, bd_ref, cd_ref, d_ref, wg_ref, bg_ref,
               out_ref, st_ref, xb_ref):
    @pl.when(pl.program_id(0) == 0)
    def _():
        st_ref[...] = jnp.zeros_like(st_ref)

    rows = bsz * S5_STEPS
    u = jnp.dot(perm_ref[0], u_ref[...].reshape(rows, S5_WIDTH),
                preferred_element_type=F32).astype(BF16)
    for h in range(2):
        xb_ref[:, 2 * S5_HALF_ST * h:2 * S5_HALF_ST * (h + 1)] = jnp.dot(
            u[:, S5_HALF_IN * h:S5_HALF_IN * (h + 1)], bd_ref[h], preferred_element_type=F32)

    for h in range(2):
        for j in range(S5_HALF_ST // S5_LANES):
            ro = 2 * S5_HALF_ST * h + S5_LANES * j
            io = ro + S5_HALF_ST
            ar = jnp.broadcast_to(a_ref[:, ro:ro + S5_LANES], (bsz, S5_LANES))
            ai = jnp.broadcast_to(a_ref[:, io:io + S5_LANES], (bsz, S5_LANES))

            def step(t, carry, ro=ro, io=io, ar=ar, ai=ai):
                xr, xi = carry
                r0 = pl.multiple_of(t * bsz, bsz)
                nr = ar * xr - ai * xi + xb_ref[pl.ds(r0, bsz), ro:ro + S5_LANES]
                ni = ar * xi + ai * xr + xb_ref[pl.ds(r0, bsz), io:io + S5_LANES]
                xb_ref[pl.ds(r0, bsz), ro:ro + S5_LANES] = nr
                xb_ref[pl.ds(r0, bsz), io:io + S5_LANES] = ni
                return nr, ni

            xr, xi = lax.fori_loop(0, S5_STEPS, step,
                                   (st_ref[:, ro:ro + S5_LANES], st_ref[:, io:io + S5_LANES]))
            st_ref[:, ro:ro + S5_LANES] = xr
            st_ref[:, io:io + S5_LANES] = xi

    ys = []
    for h in range(2):
        xh = xb_ref[:, 2 * S5_HALF_ST * h:2 * S5_HALF_ST * (h + 1)]
        ys.append(jnp.dot(xh.astype(BF16), cd_ref[h], preferred_element_type=F32))
    y = jnp.concatenate(ys, axis=1) + d_ref[...] * u.astype(F32)
    y = jax.nn.gelu(y)
    z = _dot(y, wg_ref[...]) + bg_ref[...]
    s5 = (z[:, :D_MODEL] * jax.nn.sigmoid(z[:, D_MODEL:])).astype(BF16)
    s5 = jnp.dot(perm_ref[1], s5, preferred_element_type=F32)
    gate = gs5_ref[...].reshape(rows, D_MODEL).astype(F32)
    out_ref[...] = (s5 * gate).astype(BF16).reshape(bsz, S5_STEPS, D_MODEL)


def _s5_params(lam_re, lam_im, log_dt, b_re, b_im, c_re, c_im):
    dt = jnp.exp(log_dt)[:, None]
    mag = jnp.exp(lam_re * dt)
    ab_re, ab_im = mag * jnp.cos(lam_im * dt), mag * jnp.sin(lam_im * dt)
    den = lam_re * lam_re + lam_im * lam_im
    nr, ni = ab_re - 1.0, ab_im
    f_re = (nr * lam_re + ni * lam_im) / den
    f_im = (ni * lam_re - nr * lam_im) / den
    bb_re = f_re[..., None] * b_re - f_im[..., None] * b_im
    bb_im = f_re[..., None] * b_im + f_im[..., None] * b_re
    eye = jnp.eye(S5_HALF_GROUPS, dtype=F32)

    def bd_half(bb):
        return jnp.einsum('gpc,gh->gchp', bb, eye).reshape(S5_HALF_IN, S5_HALF_ST)

    def cd_half(cc):
        return jnp.einsum('gcp,gh->gphc', cc, eye).reshape(S5_HALF_ST, S5_HALF_IN)

    bd, cd, a = [], [], []
    for h in range(2):
        sl = slice(S5_HALF_GROUPS * h, S5_HALF_GROUPS * (h + 1))
        bd.append(jnp.concatenate([bd_half(bb_re[sl]), bd_half(bb_im[sl])], axis=1))
        cd.append(jnp.concatenate([cd_half(c_re[sl]), cd_half(-c_im[sl])], axis=0))
        a += [ab_re[sl].reshape(1, S5_HALF_ST), ab_im[sl].reshape(1, S5_HALF_ST)]
    return (jnp.concatenate(a, axis=1), jnp.stack(bd).astype(BF16), jnp.stack(cd).astype(BF16))


def _s5_branch(u, gs5, a_vec, bd, cd, d_skip, w_glu_bf, b_glu, bsz, seq):
    rows = S5_STEPS * bsz
    const2 = lambda i: (0, 0)
    const3 = lambda i: (0, 0, 0)
    r_tb = jnp.arange(rows, dtype=I32)
    r_bt = (r_tb % bsz) * S5_STEPS + r_tb // bsz
    to_tb = (r_bt[:, None] == jnp.arange(rows, dtype=I32)[None, :]).astype(BF16)
    perm = jnp.stack([to_tb, to_tb.T])
    blk = lambda i: (0, i, 0)
    return pl.pallas_call(
        functools.partial(_s5_kernel, bsz),
        grid=(seq // S5_STEPS,),
        in_specs=[pl.BlockSpec((bsz, S5_STEPS, S5_WIDTH), blk),
                  pl.BlockSpec((bsz, S5_STEPS, D_MODEL), blk),
                  pl.BlockSpec((2, rows, rows), const3),
                  pl.BlockSpec((1, S5_ST_COLS), const2),
                  pl.BlockSpec((2, S5_HALF_IN, 2 * S5_HALF_ST), const3),
                  pl.BlockSpec((2, 2 * S5_HALF_ST, S5_HALF_IN), const3),
                  pl.BlockSpec((1, S5_WIDTH), const2),
                  pl.BlockSpec((S5_WIDTH, 2 * D_MODEL), const2),
                  pl.BlockSpec((1, 2 * D_MODEL), const2)],
        out_specs=pl.BlockSpec((bsz, S5_STEPS, D_MODEL), blk),
        out_shape=jax.ShapeDtypeStruct((bsz, seq, D_MODEL), BF16),
        scratch_shapes=[pltpu.VMEM((bsz, S5_ST_COLS), F32),
                        pltpu.VMEM((rows, S5_ST_COLS), F32)],
        compiler_params=_cparams("arbitrary"),
        name="s5_branch",
    )(u, gs5, perm, a_vec, bd, cd, d_skip, w_glu_bf, b_glu)


def _rwkv_kernel(p_ref, grw_ref, mu_ref, w0_ref, w2_ref, a0_ref, a2_ref, g2_ref, kk_ref,
                 ka_ref, rk_ref, lng_ref, lnb_ref, wo_ref, hsum_ref, tri_ref,
                 out_ref, st_ref, prev_ref):
    L, N, H = RW_CHUNK, RW_HEAD, RW_HEADS

    @pl.when(pl.program_id(1) == 0)
    def _():
        st_ref[...] = jnp.zeros_like(st_ref)
        prev_ref[...] = jnp.zeros_like(prev_ref)

    p = p_ref[...]
    first = lax.broadcasted_iota(I32, (L, 1), 0) == 0
    prev = jnp.where(first, prev_ref[...], pltpu.roll(p, 1, axis=0))
    prev_ref[...] = p[L - 1:L, :]
    xs = p + mu_ref[...] * (prev - p)
    c1, c2, c3 = RW_WIDTH, 2 * RW_WIDTH, 3 * RW_WIDTH
    r, k, v = xs[:, :c1], xs[:, c1:c2], xs[:, c2:c3]
    lora = xs[:, c3:c3 + 128]
    gd = xs[:, c3 + 128:]
    hsum = hsum_ref[...]

    w_log = -jax.nn.softplus(-(w0_ref[...] + _dot(jnp.tanh(lora), w2_ref[...]))) - 0.5
    e = jnp.exp(w_log)
    a = jax.nn.sigmoid(a0_ref[...] + _dot(lora, a2_ref[...]))
    g = _dot(jax.nn.sigmoid(gd), g2_ref[...])
    kk = k * kk_ref[...]
    kk = kk / jnp.maximum(jnp.sqrt(_split_dot(kk * kk, hsum)), L2_EPS)
    k = k * (1.0 + (a - 1.0) * ka_ref[...])
    avec, bvec = -kk, kk * a

    cum = _dot_split(tri_ref[...], e)
    cum_last = cum[L - 1:L, :]
    dec_in = jnp.exp(-cum)
    at = avec * jnp.exp(e - cum)
    rt = r * dec_in
    grow = jnp.exp(cum)
    bt, kt = bvec * grow, k * grow
    tail = jnp.exp(cum - cum_last)
    bw, kw = bvec * tail, k * tail
    w_all = jnp.exp(-cum_last)

    ti = lax.broadcasted_iota(I32, (L, 2 * L), 0)
    si = lax.broadcasted_iota(I32, (L, 2 * L), 1)
    lane = lax.broadcasted_iota(I32, (1, 2 * L), 1)
    left, right = (si < L), (si >= L)
    strict = jnp.where(right, si - L, si) < ti
    m_strict = [left & strict, right & strict]
    m_incl = jnp.where(right, si - L, si) <= ti
    own = [jnp.where(lane < L, 1.0, 0.0), jnp.where(lane >= L, 1.0, 0.0)]
    zeros_nk = jnp.zeros((N, 2 * N), F32)

    heads = range(H)
    par = [h % 2 for h in heads]
    slab = [slice(2 * N * (h // 2), 2 * N * (h // 2 + 1)) for h in heads]
    s_old = [st_ref[h] for h in heads]
    vm = [v[:, slab[h]] * own[par[h]] for h in heads]
    pa = []
    for h in heads:
        q, sl = par[h], slab[h]
        ar_ = jnp.concatenate([at[:, sl], rt[:, sl]], axis=0) * own[q]
        rhs = ([kt[:, sl], bt[:, sl], s_old[h], zeros_nk] if q == 0 else
               [bt[:, sl], kt[:, sl], zeros_nk, s_old[h]])
        pa.append(_dot_nt(ar_, jnp.concatenate(rhs, axis=0)))
    z = []
    for h in heads:
        q = par[h]
        top = pa[h][:L, :2 * L]
        z.append(pa[h][:L, 2 * L:] + jnp.where(m_strict[1 - q], top, 0.0)
                 + _dot(jnp.where(m_strict[q], top, 0.0),
                        jnp.concatenate([vm[h], vm[h]], axis=0)))
    for _ in range(6):
        z = [z[h] * own[par[h]]
             + _dot(z[h] * own[1 - par[h]], jnp.concatenate([z[h], z[h]], axis=0)) for h in heads]
    ys = []
    for h in heads:
        q = par[h]
        uv = [vm[h], z[h]] if q == 0 else [z[h], vm[h]]
        ys.append(pa[h][L:, 2 * L:] + _dot(jnp.where(m_incl, pa[h][L:, :2 * L], 0.0),
                                           jnp.concatenate(uv, axis=0)))
    for h in heads:
        q, sl = par[h], slab[h]
        bkw = jnp.concatenate([bw[:, sl], kw[:, sl]], axis=0) * own[q]
        upd = _dot_tn(jnp.concatenate([z[h], vm[h]], axis=0), bkw)
        st_ref[h] = s_old[h] * w_all[:, sl] + upd[N * q:N * (q + 1), :]
    y = jnp.concatenate([ys[h] + ys[h + 1] for h in range(0, H, 2)], axis=1)

    inv_n = 1.0 / N
    mean = _split_dot(y, hsum) * inv_n
    dlt = y - mean
    var = _split_dot(dlt * dlt, hsum) * inv_n
    y = dlt * lax.rsqrt(var + GN_EPS) * lng_ref[...] + lnb_ref[...]
    bonus = _split_dot(r * k * rk_ref[...], hsum) * v
    y = (y + bonus) * g
    out_ref[...] = (_dot(y, wo_ref[...]) * grw_ref[...].astype(F32)).astype(BF16)


def _rwkv_branch(prw, grw, mu, w0, w2p, a0, a2p, g2, k_k, k_a, r_k, ln_g, ln_b, wo_bf, bsz, seq):
    L = RW_CHUNK
    nc = seq // L
    row = lambda b, c: (b * nc + c, 0)
    const = lambda b, c: (0, 0)
    hsum = jnp.kron(jnp.eye(RW_HEADS, dtype=F32), jnp.ones((RW_HEAD, RW_HEAD), F32)).astype(BF16)
    tri = jnp.tril(jnp.ones((L, L), F32)).astype(BF16)
    vec = lambda n: pl.BlockSpec((1, n), const)
    return pl.pallas_call(
        _rwkv_kernel,
        grid=(bsz, nc),
        in_specs=[pl.BlockSpec((L, RW_COLS), row),
                  pl.BlockSpec((L, D_MODEL), row),
                  vec(RW_COLS), vec(RW_WIDTH),
                  pl.BlockSpec((128, RW_WIDTH), const),
                  vec(RW_WIDTH),
                  pl.BlockSpec((128, RW_WIDTH), const),
                  pl.BlockSpec((128, RW_WIDTH), const),
                  vec(RW_WIDTH), vec(RW_WIDTH), vec(RW_WIDTH), vec(RW_WIDTH), vec(RW_WIDTH),
                  pl.BlockSpec((RW_WIDTH, D_MODEL), const),
                  pl.BlockSpec((RW_WIDTH, RW_WIDTH), const),
                  pl.BlockSpec((L, L), const)],
        out_specs=pl.BlockSpec((L, D_MODEL), row),
        out_shape=jax.ShapeDtypeStruct((bsz * seq, D_MODEL), BF16),
        scratch_shapes=[pltpu.VMEM((RW_HEADS, RW_HEAD, 2 * RW_HEAD), F32),
                        pltpu.VMEM((1, RW_COLS), F32)],
        compiler_params=_cparams("arbitrary", "arbitrary"),
        name="rwkv_branch",
    )(prw, grw, mu, w0, w2p, a0, a2p, g2, k_k, k_a, r_k, ln_g, ln_b, wo_bf, hsum, tri)


ROW_TILE = 8
LANES = 128


def _store_row_tiles(ref, val):
    m = val.shape[0]
    for s in range(ROW_TILE):
        ref[pl.ds(s, m, stride=ROW_TILE), :] = val[:, LANES * s:LANES * (s + 1)]


def _load_row_tiles(ref, start, m):
    return jnp.concatenate([ref[pl.ds(start + s, m, stride=ROW_TILE), :]
                            for s in range(ROW_TILE)], axis=1)


def _mix_kernel(x_ref, s5_ref, rw_ref, wout_ref, g2_ref, rw_w_ref, rb_ref,
                h_ref, xn_ref, lg_ref):
    mixed = s5_ref[...].astype(F32) + rw_ref[...].astype(F32)
    h = x_ref[...] + _dot(mixed, wout_ref[...])
    h_ref[...] = h
    xn = h * lax.rsqrt(jnp.mean(h * h, axis=-1, keepdims=True) + RMS_EPS) * g2_ref[...]
    _store_row_tiles(xn_ref, xn)
    logits = _dot(xn, rw_w_ref[...]) + rb_ref[...]
    lg_ref[...] = logits.T[:N_EXPERTS, :]


def _mix(x2d, s5g, rwg, w_out_bf, norm2_g, router_w_pad, router_b_pad):
    ts = MIX_TILE
    t = x2d.shape[0]
    row = lambda i: (i, 0)
    const = lambda i: (0, 0)
    return pl.pallas_call(
        _mix_kernel,
        grid=(t // ts,),
        in_specs=[pl.BlockSpec((ts, D_MODEL), row),
                  pl.BlockSpec((ts, D_MODEL), row),
                  pl.BlockSpec((ts, D_MODEL), row),
                  pl.BlockSpec((D_MODEL, D_MODEL), const),
                  pl.BlockSpec((1, D_MODEL), const),
                  pl.BlockSpec((D_MODEL, LANES), const),
                  pl.BlockSpec((1, LANES), const)],
        out_specs=[pl.BlockSpec((ts, D_MODEL), row),
                   pl.BlockSpec((ts * ROW_TILE, LANES), row),
                   pl.BlockSpec((N_EXPERTS, ts), lambda i: (0, i))],
        out_shape=[jax.ShapeDtypeStruct((t, D_MODEL), F32),
                   jax.ShapeDtypeStruct((t * ROW_TILE, LANES), F32),
                   jax.ShapeDtypeStruct((N_EXPERTS, t), F32)],
        compiler_params=_cparams("parallel"),
        name="mix_out_proj",
    )(x2d, s5g, rwg, w_out_bf, norm2_g, router_w_pad, router_b_pad)


def _route_kernel(lg_ref, triu_ref, route_ref, gate_ref, cnt_ref, run_ref):
    @pl.when(pl.program_id(0) == 0)
    def _():
        run_ref[...] = jnp.zeros_like(run_ref)

    lg = lg_ref[...]
    tr = lg.shape[1]
    ie = lax.broadcasted_iota(I32, lg.shape, 0)
    sel = jnp.zeros(lg.shape, F32)
    hots, vals, idxs = [], [], []
    for _ in range(TOP_K):
        m = jnp.max(lg, axis=0, keepdims=True)
        idx = jnp.min(jnp.where(lg == m, ie, N_EXPERTS), axis=0, keepdims=True)
        hot = ie == idx
        hots.append(hot)
        vals.append(m)
        idxs.append(idx)
        sel = jnp.where(hot, 1.0, sel)
        lg = jnp.where(hot, -jnp.inf, lg)
    ex = [jnp.exp(vk - vals[0]) for vk in vals]
    den = ex[0] + ex[1] + ex[2] + ex[3]
    before = jnp.dot(sel.astype(BF16), triu_ref[...], preferred_element_type=F32) + run_ref[:, 0:1]
    ranks = [jnp.sum(jnp.where(hot, before, 0.0), axis=0, keepdims=True) for hot in hots]
    run = run_ref[...] + jnp.sum(sel, axis=1, keepdims=True)
    run_ref[...] = run
    cnt_ref[...] = run.astype(I32)
    route_ref[...] = jnp.concatenate(idxs + [rk.astype(I32) for rk in ranks], axis=0)
    gate_ref[...] = jnp.concatenate([ek / den for ek in ex] + [jnp.zeros((4, tr), F32)], axis=0)


def _route(logits_t):
    t = logits_t.shape[1]
    tr = ROUTE_TILE
    triu = jnp.triu(jnp.ones((tr, tr), F32), k=1).astype(BF16)
    return pl.pallas_call(
        _route_kernel,
        grid=(t // tr,),
        in_specs=[pl.BlockSpec((N_EXPERTS, tr), lambda i: (0, i)),
                  pl.BlockSpec((tr, tr), lambda i: (0, 0))],
        out_specs=[pl.BlockSpec((8, tr), lambda i: (0, i)),
                   pl.BlockSpec((8, tr), lambda i: (0, i)),
                   pl.BlockSpec((N_EXPERTS, 128), lambda i: (0, 0))],
        out_shape=[jax.ShapeDtypeStruct((8, t), I32),
                   jax.ShapeDtypeStruct((8, t), F32),
                   jax.ShapeDtypeStruct((N_EXPERTS, 128), I32)],
        scratch_shapes=[pltpu.VMEM((N_EXPERTS, 128), F32)],
        compiler_params=_cparams("arbitrary"),
        name="route",
    )(logits_t, triu)


def _row_copy(src, dst, sem):
    return pltpu.make_async_copy(src, dst, sem)


def _tile_rows(row):
    return pl.ds(pl.multiple_of(row * ROW_TILE, ROW_TILE), ROW_TILE)


def _dispatch_kernel(cnt_ref, pcnt_ref, poff_ref, xn_ref, route_ref, poffc_ref, xs_hbm, dest_ref,
                     dest_smem, zero_ref, sem_idx, sem_rows):
    i = pl.program_id(0)
    td = route_ref.shape[1]

    rt = route_ref[...]
    ie = lax.broadcasted_iota(I32, (N_EXPERTS, td), 0)
    poffc = poffc_ref[:, 0:1]
    rows = []
    for kk in range(TOP_K):
        base = jnp.sum(jnp.where(ie == rt[kk:kk + 1, :], poffc, 0), axis=0, keepdims=True)
        rows.append(base + rt[TOP_K + kk:TOP_K + kk + 1, :])
    dest_ref[...] = jnp.concatenate(rows + [jnp.zeros((8 - TOP_K, td), I32)], axis=0)
    idx_copy = pltpu.make_async_copy(dest_ref, dest_smem, sem_idx)
    idx_copy.start()

    @pl.when(i == 0)
    def _():
        zero_ref[...] = jnp.zeros_like(zero_ref)

        def per_expert(e, carry):
            base = poff_ref[e]

            def start(rw, c):
                _row_copy(zero_ref, xs_hbm.at[_tile_rows(base + rw)], sem_rows).start()
                return c

            def wait(rw, c):
                _row_copy(zero_ref, xs_hbm.at[_tile_rows(base + rw)], sem_rows).wait()
                return c

            lax.fori_loop(cnt_ref[e], pcnt_ref[e], start, 0)
            lax.fori_loop(cnt_ref[e], pcnt_ref[e], wait, 0)
            return carry

        lax.fori_loop(0, N_EXPERTS, per_expert, 0)

    idx_copy.wait()

    def start(tk, c):
        for kk in range(TOP_K):
            _row_copy(xn_ref.at[_tile_rows(tk)], xs_hbm.at[_tile_rows(dest_smem[kk, tk])],
                      sem_rows).start(priority=kk % 2)
        return c

    lax.fori_loop(0, td, start, 0, unroll=8)
    for _ in range(TOP_K):
        _row_copy(xn_ref, xs_hbm.at[pl.ds(0, td * ROW_TILE)], sem_rows).wait()


def _dispatch(cnt, pcnt, poff, xn_tiles, route, n_rows):
    t = route.shape[1]
    td = DISPATCH_TILE
    poffc = jnp.broadcast_to(poff[:, None], (N_EXPERTS, LANES))
    return pl.pallas_call(
        _dispatch_kernel,
        grid_spec=pltpu.PrefetchScalarGridSpec(
            num_scalar_prefetch=3,
            grid=(t // td,),
            in_specs=[pl.BlockSpec((td * ROW_TILE, LANES), lambda i, *_: (i, 0)),
                      pl.BlockSpec((8, td), lambda i, *_: (0, i)),
                      pl.BlockSpec((N_EXPERTS, LANES), lambda i, *_: (0, 0))],
            out_specs=[pl.BlockSpec(memory_space=pl.ANY),
                       pl.BlockSpec((8, td), lambda i, *_: (0, i))],
            scratch_shapes=[pltpu.SMEM((8, td), I32),
                            pltpu.VMEM((ROW_TILE, LANES), F32),
                            pltpu.SemaphoreType.DMA,
                            pltpu.SemaphoreType.DMA]),
        out_shape=[jax.ShapeDtypeStruct((n_rows * ROW_TILE, LANES), F32),
                   jax.ShapeDtypeStruct((8, t), I32)],
        compiler_params=_cparams("arbitrary"),
        name="dispatch",
    )(cnt, pcnt, poff, xn_tiles, route, poffc)


DEINT = 256


def _expert_kernel(be_ref, nu_ref, xs_ref, w1_ref, b1g_ref, b1l_ref, w2_ref, b2_ref, sel_ref,
                   ys_ref, w1g_ref, w1l_ref, w2b_ref):
    i = pl.program_id(0)
    new_expert = jnp.logical_or(i == 0, be_ref[i] != be_ref[jnp.maximum(i - 1, 0)])

    @pl.when(jnp.logical_and(new_expert, i < nu_ref[0]))
    def _():
        half = DEINT // 2
        for c in range(2 * D_MODEL // DEINT):
            cols = w1_ref[0, :, DEINT * c:DEINT * (c + 1)].astype(BF16)
            sep = jnp.dot(cols, sel_ref[...], preferred_element_type=F32)
            w1g_ref[:, half * c:half * (c + 1)] = sep[:, :half].astype(BF16)
            w1l_ref[:, half * c:half * (c + 1)] = sep[:, half:].astype(BF16)
        w2b_ref[...] = w2_ref[0].astype(BF16)

    @pl.when(i < nu_ref[0])
    def _():
        x = _load_row_tiles(xs_ref, 0, MOE_ROWS).astype(BF16)
        hg = jnp.dot(x, w1g_ref[...], preferred_element_type=F32) + b1g_ref[0]
        hl = jnp.dot(x, w1l_ref[...], preferred_element_type=F32) + b1l_ref[0]
        x_glu = jnp.minimum(hg, SWIGLU_LIMIT)
        x_lin = jnp.clip(hl, -SWIGLU_LIMIT, SWIGLU_LIMIT)
        act = x_glu * jax.nn.sigmoid(SWIGLU_ALPHA * x_glu) * (x_lin + 1.0)
        _store_row_tiles(ys_ref, jnp.dot(act.astype(BF16), w2b_ref[...],
                                         preferred_element_type=F32) + b2_ref[0])

    @pl.when(i >= nu_ref[0])
    def _():
        ys_ref[...] = jnp.zeros_like(ys_ref)


def _experts(blk_exp, n_used, xs, w1, b1g, b1l, w2, b2):
    n_rows = xs.shape[0] // ROW_TILE
    tm = MOE_ROWS
    wmap = lambda i, be, nu: (be[i], 0, 0)
    j = jnp.arange(DEINT, dtype=I32)
    sel = (jnp.where(j % 2 == 0, j // 2, DEINT // 2 + j // 2)[:, None] == j[None, :]).astype(BF16)
    return pl.pallas_call(
        _expert_kernel,
        grid_spec=pltpu.PrefetchScalarGridSpec(
            num_scalar_prefetch=2,
            grid=(n_rows // tm,),
            in_specs=[pl.BlockSpec((tm * ROW_TILE, LANES),
                                   lambda i, be, nu: (jnp.minimum(i, nu[0] - 1), 0)),
                      pl.BlockSpec((1, D_MODEL, 2 * D_MODEL), wmap),
                      pl.BlockSpec((1, 1, D_MODEL), wmap),
                      pl.BlockSpec((1, 1, D_MODEL), wmap),
                      pl.BlockSpec((1, D_MODEL, D_MODEL), wmap),
                      pl.BlockSpec((1, 1, D_MODEL), wmap),
                      pl.BlockSpec((DEINT, DEINT), lambda i, be, nu: (0, 0))],
            out_specs=pl.BlockSpec((tm * ROW_TILE, LANES), lambda i, be, nu: (i, 0)),
            scratch_shapes=[pltpu.VMEM((D_MODEL, D_MODEL), BF16),
                            pltpu.VMEM((D_MODEL, D_MODEL), BF16),
                            pltpu.VMEM((D_MODEL, D_MODEL), BF16)]),
        out_shape=jax.ShapeDtypeStruct((n_rows * ROW_TILE, LANES), F32),
        compiler_params=pltpu.CompilerParams(dimension_semantics=("arbitrary",),
                                             vmem_limit_bytes=EXPERT_VMEM_LIMIT_BYTES),
        name="experts",
    )(blk_exp, n_used, xs, w1, b1g, b1l, w2, b2, sel)


def _combine_kernel(h_ref, gate_ref, nf_ref, dest_hbm, ys_hbm, out_ref,
                    dest_smem, buf_ref, sem_idx, sem_rows):
    i = pl.program_id(0)
    tc = h_ref.shape[0]

    idx_copy = pltpu.make_async_copy(dest_hbm.at[:, pl.ds(i * tc, tc)], dest_smem, sem_idx)
    idx_copy.start()
    idx_copy.wait()

    def start(tk, c):
        for kk in range(TOP_K):
            _row_copy(ys_hbm.at[_tile_rows(dest_smem[kk, tk])],
                      buf_ref.at[_tile_rows(kk * tc + tk)], sem_rows).start(priority=kk % 2)
        return c

    lax.fori_loop(0, tc, start, 0, unroll=8)
    for kk in range(TOP_K):
        _row_copy(ys_hbm.at[pl.ds(0, tc * ROW_TILE)],
                  buf_ref.at[pl.ds(kk * tc * ROW_TILE, tc * ROW_TILE)], sem_rows).wait()

    gate = gate_ref[...]
    acc = jnp.zeros(h_ref.shape, F32)
    for kk in range(TOP_K):
        acc = acc + gate[:, kk:kk + 1] * _load_row_tiles(buf_ref, kk * tc * ROW_TILE, tc)
    h = h_ref[...] + acc
    out_ref[...] = h * lax.rsqrt(jnp.mean(h * h, axis=-1, keepdims=True) + RMS_EPS) * nf_ref[...]


def _combine(h, gates_tok, norm_f, dest, ys):
    t = h.shape[0]
    tc = COMBINE_TILE
    return pl.pallas_call(
        _combine_kernel,
        grid=(t // tc,),
        in_specs=[pl.BlockSpec((tc, D_MODEL), lambda i: (i, 0)),
                  pl.BlockSpec((tc, TOP_K), lambda i: (i, 0)),
                  pl.BlockSpec((1, D_MODEL), lambda i: (0, 0)),
                  pl.BlockSpec(memory_space=pl.ANY),
                  pl.BlockSpec(memory_space=pl.ANY)],
        out_specs=pl.BlockSpec((tc, D_MODEL), lambda i: (i, 0)),
        scratch_shapes=[pltpu.SMEM((8, tc), I32),
                        pltpu.VMEM((TOP_K * tc * ROW_TILE, LANES), F32),
                        pltpu.SemaphoreType.DMA,
                        pltpu.SemaphoreType.DMA],
        out_shape=jax.ShapeDtypeStruct((t, D_MODEL), F32),
        compiler_params=_cparams("arbitrary"),
        name="combine",
    )(h, gates_tok, norm_f, dest, ys)


def _moe_plan(counts, n_blocks):
    pcnt = ((counts + MOE_ROWS - 1) // MOE_ROWS) * MOE_ROWS
    pend = jnp.cumsum(pcnt)
    poff = pend - pcnt
    n_used = (pend[-1:] // MOE_ROWS).astype(I32)
    blk_start = jnp.arange(n_blocks, dtype=I32) * MOE_ROWS
    blk_exp = jnp.minimum(jnp.sum(pend[None, :] <= blk_start[:, None], axis=1),
                          N_EXPERTS - 1).astype(I32)
    return pcnt.astype(I32), poff.astype(I32), n_used, blk_exp


def kernel(x, norm1_g, w_in, s5_lambda_re, s5_lambda_im, s5_log_dt, s5_b_re, s5_b_im, s5_c_re, s5_c_im, s5_d, s5_w_glu, s5_b_glu, rw_mu, rw_w0, rw_w2, rw_a0, rw_a2, rw_g2, rw_k_k, rw_k_a, rw_r_k, rw_ln_g, rw_ln_b, rw_w_out, w_out, norm2_g, router_w, router_b, moe_w1, moe_b1, moe_w2, moe_b2, norm_f_g):
    bsz, seq, d = x.shape
    t = bsz * seq
    assert d == D_MODEL and norm1_g.shape[0] == 1
    assert seq % S5_STEPS == 0 and seq % RW_CHUNK == 0 and bsz % 8 == 0
    assert t % IN_TILE == 0 and t % MIX_TILE == 0 and t % ROUTE_TILE == 0
    assert t % DISPATCH_TILE == 0 and t % COMBINE_TILE == 0
    x2d = x.reshape(t, d)

    u, prw, gs5, grw = _in_proj(x2d, norm1_g, w_in[0].astype(BF16), bsz, seq)
    a_vec, bd, cd = _s5_params(s5_lambda_re[0], s5_lambda_im[0], s5_log_dt[0], s5_b_re[0],
                               s5_b_im[0], s5_c_re[0], s5_c_im[0])
    s5g = _s5_branch(u.reshape(bsz, seq, S5_WIDTH), gs5.reshape(bsz, seq, D_MODEL),
                     a_vec, bd, cd, s5_d, s5_w_glu[0].astype(BF16), s5_b_glu, bsz, seq)
    zpad = jnp.zeros((64, RW_WIDTH), F32)
    w2p = jnp.concatenate([rw_w2[0], zpad], axis=0).astype(BF16)
    a2p = jnp.concatenate([zpad, rw_a2[0]], axis=0).astype(BF16)
    rwg = _rwkv_branch(prw, grw, rw_mu, rw_w0, w2p, rw_a0, a2p, rw_g2[0].astype(BF16), rw_k_k,
                       rw_k_a, rw_r_k.reshape(1, RW_WIDTH), rw_ln_g, rw_ln_b,
                       rw_w_out[0].astype(BF16), bsz, seq)
    lane_pad = ((0, 0), (0, LANES - N_EXPERTS))
    h, xn2, logits_t = _mix(x2d, s5g.reshape(t, D_MODEL), rwg, w_out[0].astype(BF16), norm2_g,
                            jnp.pad(router_w[0], lane_pad).astype(BF16),
                            jnp.pad(router_b, lane_pad))

    route, gates, cnt2d = _route(logits_t)
    counts = cnt2d[:, 0]
    n_blocks = (t * TOP_K) // MOE_ROWS + N_EXPERTS
    pcnt, poff, n_used, blk_exp = _moe_plan(counts, n_blocks)
    xs, dest = _dispatch(counts, pcnt, poff, xn2, route, n_blocks * MOE_ROWS)
    ys = _experts(blk_exp, n_used, xs, moe_w1[0],
                  moe_b1[0][:, None, 0::2], moe_b1[0][:, None, 1::2],
                  moe_w2[0], moe_b2[0][:, None, :])
    out = _combine(h, gates[:TOP_K].T, norm_f_g.reshape(1, D_MODEL), dest, ys)
    return out.reshape(bsz, seq, d)
```

```python
import functools
import math

import jax
import jax.numpy as jnp
from jax import lax
from jax.experimental import pallas as pl
from jax.experimental.pallas import tpu as pltpu

F32 = jnp.float32
BF16 = jnp.bfloat16
I32 = jnp.int32

D_MODEL = 1024
S5_WIDTH = 512
S5_GROUP = 16
S5_GROUPS = 32
S5_STATE = 64
S5_HALF_GROUPS = 16
S5_HALF_IN = S5_HALF_GROUPS * S5_GROUP
S5_HALF_ST = S5_HALF_GROUPS * S5_STATE
S5_ST_COLS = 4 * S5_HALF_ST
RW_WIDTH = 512
RW_HEAD = 64
RW_HEADS = 8
RW_COLS = 1792
IN_COLS = 4352
N_EXPERTS = 32
TOP_K = 4
SWIGLU_ALPHA = 1.702
SWIGLU_LIMIT = 7.0
RMS_EPS = 1e-5
GN_EPS = 64e-5
L2_EPS = 1e-12

VMEM_LIMIT_BYTES = 48 * 1024 * 1024
EXPERT_VMEM_LIMIT_BYTES = 56 * 1024 * 1024

IN_TILE = 256
MIX_TILE = 512
S5_STEPS = 32
S5_LANES = 512
RW_CHUNK = 64
RW_BATCH = 2
ROUTE_TILE = 512
MOE_ROWS = 512
DISPATCH_TILE = 512
COMBINE_TILE = 256


def _cparams(*sem):
    return pltpu.CompilerParams(dimension_semantics=sem, vmem_limit_bytes=VMEM_LIMIT_BYTES)


def _dot(a, b):
    return jnp.dot(a.astype(BF16), b.astype(BF16), preferred_element_type=F32)


def _dot_nt(a, b):
    return lax.dot_general(a.astype(BF16), b.astype(BF16), (((1,), (1,)), ((), ())),
                           preferred_element_type=F32)


def _dot_tn(a, b):
    return lax.dot_general(a.astype(BF16), b.astype(BF16), (((0,), (0,)), ((), ())),
                           preferred_element_type=F32)


def _dot_split(a, b_f32):
    hi = b_f32.astype(BF16)
    lo = (b_f32 - hi.astype(F32)).astype(BF16)
    return (jnp.dot(a, hi, preferred_element_type=F32)
            + jnp.dot(a, lo, preferred_element_type=F32))


def _split_dot(a_f32, b):
    hi = a_f32.astype(BF16)
    lo = (a_f32 - hi.astype(F32)).astype(BF16)
    return (jnp.dot(hi, b, preferred_element_type=F32)
            + jnp.dot(lo, b, preferred_element_type=F32))


def _in_proj_kernel(x_ref, g_ref, w_ref, u_ref, prw_ref, gs5_ref, grw_ref):
    x = x_ref[...]
    xn = x * lax.rsqrt(jnp.mean(x * x, axis=-1, keepdims=True) + RMS_EPS) * g_ref[...]
    p = _dot(xn, w_ref[...])
    c0, c1, c2 = S5_WIDTH, S5_WIDTH + RW_COLS, S5_WIDTH + RW_COLS + D_MODEL
    u_ref[...] = p[:, :c0].astype(BF16)
    prw_ref[...] = p[:, c0:c1]
    gs5_ref[...] = jax.nn.sigmoid(p[:, c1:c2]).astype(BF16)
    grw_ref[...] = jax.nn.sigmoid(p[:, c2:]).astype(BF16)


def _in_proj(x2d, norm_g, w_in_bf, bsz, seq):
    ts = IN_TILE
    t = bsz * seq
    row = lambda i: (i, 0)
    return pl.pallas_call(
        _in_proj_kernel,
        grid=(t // ts,),
        in_specs=[pl.BlockSpec((ts, D_MODEL), row),
                  pl.BlockSpec((1, D_MODEL), lambda i: (0, 0)),
                  pl.BlockSpec((D_MODEL, IN_COLS), lambda i: (0, 0))],
        out_specs=[pl.BlockSpec((ts, S5_WIDTH), row),
                   pl.BlockSpec((ts, RW_COLS), row),
                   pl.BlockSpec((ts, D_MODEL), row),
                   pl.BlockSpec((ts, D_MODEL), row)],
        out_shape=[jax.ShapeDtypeStruct((t, S5_WIDTH), BF16),
                   jax.ShapeDtypeStruct((t, RW_COLS), F32),
                   jax.ShapeDtypeStruct((t, D_MODEL), BF16),
                   jax.ShapeDtypeStruct((t, D_MODEL), BF16)],
        compiler_params=_cparams("parallel"),
        name="in_proj",
    )(x2d, norm_g, w_in_bf)


def _s5_kernel(bsz, u_ref, gs5_ref, perm_ref, a_ref, bd_ref, cd_ref, d_ref, wg_ref, bg_ref,
               out_ref, st_ref, xb_ref):
    @pl.when(pl.program_id(0) == 0)
    def _():
        st_ref[...] = jnp.zeros_like(st_ref)

    rows = bsz * S5_STEPS
    u = jnp.dot(perm_ref[0], u_ref[...].reshape(rows, S5_WIDTH),
                preferred_element_type=F32).astype(BF16)
    for h in range(2):
        xb_ref[:, 2 * S5_HALF_ST * h:2 * S5_HALF_ST * (h + 1)] = jnp.dot(
            u[:, S5_HALF_IN * h:S5_HALF_IN * (h + 1)], bd_ref[h], preferred_element_type=F32)

    for h in range(2):
        for j in range(S5_HALF_ST // S5_LANES):
            ro = 2 * S5_HALF_ST * h + S5_LANES * j
            io = ro + S5_HALF_ST
            ar = jnp.broadcast_to(a_ref[:, ro:ro + S5_LANES], (bsz, S5_LANES))
            ai = jnp.broadcast_to(a_ref[:, io:io + S5_LANES], (bsz, S5_LANES))

            def step(t, carry, ro=ro, io=io, ar=ar, ai=ai):
                xr, xi = carry
                r0 = pl.multiple_of(t * bsz, bsz)
                nr = ar * xr - ai * xi + xb_ref[pl.ds(r0, bsz), ro:ro + S5_LANES]
                ni = ar * xi + ai * xr + xb_ref[pl.ds(r0, bsz), io:io + S5_LANES]
                xb_ref[pl.ds(r0, bsz), ro:ro + S5_LANES] = nr
                xb_ref[pl.ds(r0, bsz), io:io + S5_LANES] = ni
                return nr, ni

            xr, xi = lax.fori_loop(0, S5_STEPS, step,
                                   (st_ref[:, ro:ro + S5_LANES], st_ref[:, io:io + S5_LANES]))
            st_ref[:, ro:ro + S5_LANES] = xr
            st_ref[:, io:io + S5_LANES] = xi

    ys = []
    for h in range(2):
        xh = xb_ref[:, 2 * S5_HALF_ST * h:2 * S5_HALF_ST * (h + 1)]
        ys.append(jnp.dot(xh.astype(BF16), cd_ref[h], preferred_element_type=F32))
    y = jnp.concatenate(ys, axis=1) + d_ref[...] * u.astype(F32)
    y = jax.nn.gelu(y)
    z = _dot(y, wg_ref[...]) + bg_ref[...]
    s5 = (z[:, :D_MODEL] * jax.nn.sigmoid(z[:, D_MODEL:])).astype(BF16)
    s5 = jnp.dot(perm_ref[1], s5, preferred_element_type=F32)
    gate = gs5_ref[...].reshape(rows, D_MODEL).astype(F32)
    out_ref[...] = (s5 * gate).astype(BF16).reshape(bsz, S5_STEPS, D_MODEL)


def _s5_params(lam_re, lam_im, log_dt, b_re, b_im, c_re, c_im):
    dt = jnp.exp(log_dt)[:, None]
    mag = jnp.exp(lam_re * dt)
    ab_re, ab_im = mag * jnp.cos(lam_im * dt), mag * jnp.sin(lam_im * dt)
    den = lam_re * lam_re + lam_im * lam_im
    nr, ni = ab_re - 1.0, ab_im
    f_re = (nr * lam_re + ni * lam_im) / den
    f_im = (ni * lam_re - nr * lam_im) / den
    bb_re = f_re[..., None] * b_re - f_im[..., None] * b_im
    bb_im = f_re[..., None] * b_im + f_im[..., None] * b_re
    eye = jnp.eye(S5_HALF_GROUPS, dtype=F32)

    def bd_half(bb):
        return jnp.einsum('gpc,gh->gchp', bb, eye).reshape(S5_HALF_IN, S5_HALF_ST)

    def cd_half(cc):
        return jnp.einsum('gcp,gh->gphc', cc, eye).reshape(S5_HALF_ST, S5_HALF_IN)

    bd, cd, a = [], [], []
    for h in range(2):
        sl = slice(S5_HALF_GROUPS * h, S5_HALF_GROUPS * (h + 1))
        bd.append(jnp.concatenate([bd_half(bb_re[sl]), bd_half(bb_im[sl])], axis=1))
        cd.append(jnp.concatenate([cd_half(c_re[sl]), cd_half(-c_im[sl])], axis=0))
        a += [ab_re[sl].reshape(1, S5_HALF_ST), ab_im[sl].reshape(1, S5_HALF_ST)]
    return (jnp.concatenate(a, axis=1), jnp.stack(bd).astype(BF16), jnp.stack(cd).astype(BF16))


def _s5_branch(u, gs5, a_vec, bd, cd, d_skip, w_glu_bf, b_glu, bsz, seq):
    rows = S5_STEPS * bsz
    const2 = lambda i: (0, 0)
    const3 = lambda i: (0, 0, 0)
    r_tb = jnp.arange(rows, dtype=I32)
    r_bt = (r_tb % bsz) * S5_STEPS + r_tb // bsz
    to_tb = (r_bt[:, None] == jnp.arange(rows, dtype=I32)[None, :]).astype(BF16)
    perm = jnp.stack([to_tb, to_tb.T])
    blk = lambda i: (0, i, 0)
    return pl.pallas_call(
        functools.partial(_s5_kernel, bsz),
        grid=(seq // S5_STEPS,),
        in_specs=[pl.BlockSpec((bsz, S5_STEPS, S5_WIDTH), blk),
                  pl.BlockSpec((bsz, S5_STEPS, D_MODEL), blk),
                  pl.BlockSpec((2, rows, rows), const3),
                  pl.BlockSpec((1, S5_ST_COLS), const2),
                  pl.BlockSpec((2, S5_HALF_IN, 2 * S5_HALF_ST), const3),
                  pl.BlockSpec((2, 2 * S5_HALF_ST, S5_HALF_IN), const3),
                  pl.BlockSpec((1, S5_WIDTH), const2),
                  pl.BlockSpec((S5_WIDTH, 2 * D_MODEL), const2),
                  pl.BlockSpec((1, 2 * D_MODEL), const2)],
        out_specs=pl.BlockSpec((bsz, S5_STEPS, D_MODEL), blk),
        out_shape=jax.ShapeDtypeStruct((bsz, seq, D_MODEL), BF16),
        scratch_shapes=[pltpu.VMEM((bsz, S5_ST_COLS), F32),
                        pltpu.VMEM((rows, S5_ST_COLS), F32)],
        compiler_params=_cparams("arbitrary"),
        name="s5_branch",
    )(u, gs5, perm, a_vec, bd, cd, d_skip, w_glu_bf, b_glu)


def _rwkv_kernel(p_ref, grw_ref, mu_ref, w0_ref, w2_ref, a0_ref, a2_ref, g2_ref, kk_ref,
                 ka_ref, rk_ref, lng_ref, lnb_ref, wo_ref, hsum_ref, tri_ref,
                 out_ref, st_ref, prev_ref):
    L, N, H, NB = RW_CHUNK, RW_HEAD, RW_HEADS, RW_BATCH
    rows = NB * L

    @pl.when(pl.program_id(1) == 0)
    def _():
        st_ref[...] = jnp.zeros_like(st_ref)
        prev_ref[...] = jnp.zeros_like(prev_ref)

    p = p_ref[...].reshape(rows, RW_COLS)
    rid = lax.broadcasted_iota(I32, (rows, 1), 0)
    prev = pltpu.roll(p, 1, axis=0)
    for b in range(NB):
        prev = jnp.where(rid == b * L, prev_ref[b:b + 1, :], prev)
        prev_ref[b:b + 1, :] = p[(b + 1) * L - 1:(b + 1) * L, :]
    xs = p + mu_ref[...] * (prev - p)
    c1, c2, c3 = RW_WIDTH, 2 * RW_WIDTH, 3 * RW_WIDTH
    r, k, v = xs[:, :c1], xs[:, c1:c2], xs[:, c2:c3]
    lora = xs[:, c3:c3 + 128]
    gd = xs[:, c3 + 128:]
    hsum = hsum_ref[...]

    w_log = -jax.nn.softplus(-(w0_ref[...] + _dot(jnp.tanh(lora), w2_ref[...]))) - 0.5
    e = jnp.exp(w_log)
    a = jax.nn.sigmoid(a0_ref[...] + _dot(lora, a2_ref[...]))
    g = _dot(jax.nn.sigmoid(gd), g2_ref[...])
    kk = k * kk_ref[...]
    kk = kk / jnp.maximum(jnp.sqrt(_split_dot(kk * kk, hsum)), L2_EPS)
    k = k * (1.0 + (a - 1.0) * ka_ref[...])
    avec, bvec = -kk, kk * a

    cum = _dot_split(tri_ref[...], e)
    cum_last = jnp.concatenate(
        [jnp.broadcast_to(cum[(b + 1) * L - 1:(b + 1) * L, :], (L, RW_WIDTH)) for b in range(NB)],
        axis=0)
    dec_in = jnp.exp(-cum)
    at = avec * jnp.exp(e - cum)
    rt = r * dec_in
    grow = jnp.exp(cum)
    bt, kt = bvec * grow, k * grow
    tail = jnp.exp(cum - cum_last)
    bw, kw = bvec * tail, k * tail
    w_all = jnp.exp(-cum_last)

    ti = lax.broadcasted_iota(I32, (L, 2 * L), 0)
    si = lax.broadcasted_iota(I32, (L, 2 * L), 1)
    lane = lax.broadcasted_iota(I32, (1, 2 * L), 1)
    left, right = (si < L), (si >= L)
    strict = jnp.where(right, si - L, si) < ti
    m_strict = [left & strict, right & strict]
    m_incl = jnp.where(right, si - L, si) <= ti
    own = [jnp.where(lane < L, 1.0, 0.0), jnp.where(lane >= L, 1.0, 0.0)]
    zeros_nk = jnp.zeros((N, 2 * N), F32)

    chains = range(NB * H)
    par = [c % 2 for c in chains]
    rb = [slice(L * (c // H), L * (c // H + 1)) for c in chains]
    slab = [slice(2 * N * ((c % H) // 2), 2 * N * ((c % H) // 2 + 1)) for c in chains]
    s_old = [st_ref[c] for c in chains]
    vm = [v[rb[c], slab[c]] * own[par[c]] for c in chains]
    pa = []
    for c in chains:
        q, r_, sl = par[c], rb[c], slab[c]
        ar_ = jnp.concatenate([at[r_, sl], rt[r_, sl]], axis=0) * own[q]
        rhs = ([kt[r_, sl], bt[r_, sl], s_old[c], zeros_nk] if q == 0 else
               [bt[r_, sl], kt[r_, sl], zeros_nk, s_old[c]])
        pa.append(_dot_nt(ar_, jnp.concatenate(rhs, axis=0)))
    z = []
    for c in chains:
        q = par[c]
        top = pa[c][:L, :2 * L]
        z.append(pa[c][:L, 2 * L:] + jnp.where(m_strict[1 - q], top, 0.0)
                 + _dot(jnp.where(m_strict[q], top, 0.0),
                        jnp.concatenate([vm[c], vm[c]], axis=0)))
    for _ in range(6):
        z = [z[c] * own[par[c]]
             + _dot(z[c] * own[1 - par[c]], jnp.concatenate([z[c], z[c]], axis=0)) for c in chains]
    ys = []
    for c in chains:
        q = par[c]
        uv = [vm[c], z[c]] if q == 0 else [z[c], vm[c]]
        ys.append(pa[c][L:, 2 * L:] + _dot(jnp.where(m_incl, pa[c][L:, :2 * L], 0.0),
                                           jnp.concatenate(uv, axis=0)))
    for c in chains:
        q, r_, sl = par[c], rb[c], slab[c]
        bkw = jnp.concatenate([bw[r_, sl], kw[r_, sl]], axis=0) * own[q]
        upd = _dot_tn(jnp.concatenate([z[c], vm[c]], axis=0), bkw)
        st_ref[c] = s_old[c] * w_all[L * (c // H):L * (c // H) + 1, sl] + upd[N * q:N * (q + 1), :]
    y = jnp.concatenate(
        [jnp.concatenate([ys[b * H + h] + ys[b * H + h + 1] for h in range(0, H, 2)], axis=1)
         for b in range(NB)], axis=0)

    inv_n = 1.0 / N
    mean = _split_dot(y, hsum) * inv_n
    dlt = y - mean
    var = _split_dot(dlt * dlt, hsum) * inv_n
    y = dlt * lax.rsqrt(var + GN_EPS) * lng_ref[...] + lnb_ref[...]
    bonus = _split_dot(r * k * rk_ref[...], hsum) * v
    y = (y + bonus) * g
    gate = grw_ref[...].reshape(rows, D_MODEL).astype(F32)
    out_ref[...] = (_dot(y, wo_ref[...]) * gate).astype(BF16).reshape(NB, L, D_MODEL)


def _rwkv_branch(prw, grw, mu, w0, w2p, a0, a2p, g2, k_k, k_a, r_k, ln_g, ln_b, wo_bf, bsz, seq):
    L, nb = RW_CHUNK, RW_BATCH
    blk = lambda b, c: (b, c, 0)
    const = lambda b, c: (0, 0)
    hsum = jnp.kron(jnp.eye(RW_HEADS, dtype=F32), jnp.ones((RW_HEAD, RW_HEAD), F32)).astype(BF16)
    tri = jnp.kron(jnp.eye(nb, dtype=F32), jnp.tril(jnp.ones((L, L), F32))).astype(BF16)
    vec = lambda n: pl.BlockSpec((1, n), const)
    return pl.pallas_call(
        _rwkv_kernel,
        grid=(bsz // nb, seq // L),
        in_specs=[pl.BlockSpec((nb, L, RW_COLS), blk),
                  pl.BlockSpec((nb, L, D_MODEL), blk),
                  vec(RW_COLS), vec(RW_WIDTH),
                  pl.BlockSpec((128, RW_WIDTH), const),
                  vec(RW_WIDTH),
                  pl.BlockSpec((128, RW_WIDTH), const),
                  pl.BlockSpec((128, RW_WIDTH), const),
                  vec(RW_WIDTH), vec(RW_WIDTH), vec(RW_WIDTH), vec(RW_WIDTH), vec(RW_WIDTH),
                  pl.BlockSpec((RW_WIDTH, D_MODEL), const),
                  pl.BlockSpec((RW_WIDTH, RW_WIDTH), const),
                  pl.BlockSpec((nb * L, nb * L), const)],
        out_specs=pl.BlockSpec((nb, L, D_MODEL), blk),
        out_shape=jax.ShapeDtypeStruct((bsz, seq, D_MODEL), BF16),
        scratch_shapes=[pltpu.VMEM((nb * RW_HEADS, RW_HEAD, 2 * RW_HEAD), F32),
                        pltpu.VMEM((nb, RW_COLS), F32)],
        compiler_params=_cparams("arbitrary", "arbitrary"),
        name="rwkv_branch",
    )(prw, grw, mu, w0, w2p, a0, a2p, g2, k_k, k_a, r_k, ln_g, ln_b, wo_bf, hsum, tri)


ROW_TILE = 8
LANES = 128


def _store_row_tiles(ref, val):
    m = val.shape[0]
    for s in range(ROW_TILE):
        ref[pl.ds(s, m, stride=ROW_TILE), :] = val[:, LANES * s:LANES * (s + 1)]


def _load_row_tiles(ref, start, m):
    return jnp.concatenate([ref[pl.ds(start + s, m, stride=ROW_TILE), :]
                            for s in range(ROW_TILE)], axis=1)


def _mix_kernel(x_ref, s5_ref, rw_ref, wout_ref, g2_ref, rw_w_ref, rb_ref,
                h_ref, xn_ref, lg_ref):
    mixed = s5_ref[...].astype(F32) + rw_ref[...].astype(F32)
    h = x_ref[...] + _dot(mixed, wout_ref[...])
    h_ref[...] = h
    xn = h * lax.rsqrt(jnp.mean(h * h, axis=-1, keepdims=True) + RMS_EPS) * g2_ref[...]
    _store_row_tiles(xn_ref, xn)
    logits = _dot(xn, rw_w_ref[...]) + rb_ref[...]
    lg_ref[...] = logits.T[:N_EXPERTS, :]


def _mix(x2d, s5g, rwg, w_out_bf, norm2_g, router_w_pad, router_b_pad):
    ts = MIX_TILE
    t = x2d.shape[0]
    row = lambda i: (i, 0)
    const = lambda i: (0, 0)
    return pl.pallas_call(
        _mix_kernel,
        grid=(t // ts,),
        in_specs=[pl.BlockSpec((ts, D_MODEL), row),
                  pl.BlockSpec((ts, D_MODEL), row),
                  pl.BlockSpec((ts, D_MODEL), row),
                  pl.BlockSpec((D_MODEL, D_MODEL), const),
                  pl.BlockSpec((1, D_MODEL), const),
                  pl.BlockSpec((D_MODEL, LANES), const),
                  pl.BlockSpec((1, LANES), const)],
        out_specs=[pl.BlockSpec((ts, D_MODEL), row),
                   pl.BlockSpec((ts * ROW_TILE, LANES), row),
                   pl.BlockSpec((N_EXPERTS, ts), lambda i: (0, i))],
        out_shape=[jax.ShapeDtypeStruct((t, D_MODEL), F32),
                   jax.ShapeDtypeStruct((t * ROW_TILE, LANES), F32),
                   jax.ShapeDtypeStruct((N_EXPERTS, t), F32)],
        compiler_params=_cparams("parallel"),
        name="mix_out_proj",
    )(x2d, s5g, rwg, w_out_bf, norm2_g, router_w_pad, router_b_pad)


def _route_kernel(lg_ref, triu_ref, route_ref, gate_ref, cnt_ref, run_ref):
    @pl.when(pl.program_id(0) == 0)
    def _():
        run_ref[...] = jnp.zeros_like(run_ref)

    lg = lg_ref[...]
    tr = lg.shape[1]
    ie = lax.broadcasted_iota(I32, lg.shape, 0)
    sel = jnp.zeros(lg.shape, F32)
    hots, vals, idxs = [], [], []
    for _ in range(TOP_K):
        m = jnp.max(lg, axis=0, keepdims=True)
        idx = jnp.min(jnp.where(lg == m, ie, N_EXPERTS), axis=0, keepdims=True)
        hot = ie == idx
        hots.append(hot)
        vals.append(m)
        idxs.append(idx)
        sel = jnp.where(hot, 1.0, sel)
        lg = jnp.where(hot, -jnp.inf, lg)
    ex = [jnp.exp(vk - vals[0]) for vk in vals]
    den = ex[0] + ex[1] + ex[2] + ex[3]
    before = jnp.dot(sel.astype(BF16), triu_ref[...], preferred_element_type=F32) + run_ref[:, 0:1]
    ranks = [jnp.sum(jnp.where(hot, before, 0.0), axis=0, keepdims=True) for hot in hots]
    run = run_ref[...] + jnp.sum(sel, axis=1, keepdims=True)
    run_ref[...] = run
    cnt_ref[...] = run.astype(I32)
    route_ref[...] = jnp.concatenate(idxs + [rk.astype(I32) for rk in ranks], axis=0)
    gate_ref[...] = jnp.concatenate([ek / den for ek in ex] + [jnp.zeros((4, tr), F32)], axis=0)


def _route(logits_t):
    t = logits_t.shape[1]
    tr = ROUTE_TILE
    triu = jnp.triu(jnp.ones((tr, tr), F32), k=1).astype(BF16)
    return pl.pallas_call(
        _route_kernel,
        grid=(t // tr,),
        in_specs=[pl.BlockSpec((N_EXPERTS, tr), lambda i: (0, i)),
                  pl.BlockSpec((tr, tr), lambda i: (0, 0))],
        out_specs=[pl.BlockSpec((8, tr), lambda i: (0, i)),
                   pl.BlockSpec((8, tr), lambda i: (0, i)),
                   pl.BlockSpec((N_EXPERTS, 128), lambda i: (0, 0))],
        out_shape=[jax.ShapeDtypeStruct((8, t), I32),
                   jax.ShapeDtypeStruct((8, t), F32),
                   jax.ShapeDtypeStruct((N_EXPERTS, 128), I32)],
        scratch_shapes=[pltpu.VMEM((N_EXPERTS, 128), F32)],
        compiler_params=_cparams("arbitrary"),
        name="route",
    )(logits_t, triu)


def _row_copy(src, dst, sem):
    return pltpu.make_async_copy(src, dst, sem)


def _tile_rows(row):
    return pl.ds(pl.multiple_of(row * ROW_TILE, ROW_TILE), ROW_TILE)


def _dispatch_kernel(cnt_ref, pcnt_ref, poff_ref, xn_ref, route_ref, poffc_ref, xs_hbm, dest_ref,
                     dest_smem, zero_ref, sem_idx, sem_rows):
    i = pl.program_id(0)
    td = route_ref.shape[1]

    rt = route_ref[...]
    ie = lax.broadcasted_iota(I32, (N_EXPERTS, td), 0)
    poffc = poffc_ref[:, 0:1]
    rows = []
    for kk in range(TOP_K):
        base = jnp.sum(jnp.where(ie == rt[kk:kk + 1, :], poffc, 0), axis=0, keepdims=True)
        rows.append(base + rt[TOP_K + kk:TOP_K + kk + 1, :])
    dest_ref[...] = jnp.concatenate(rows + [jnp.zeros((8 - TOP_K, td), I32)], axis=0)
    idx_copy = pltpu.make_async_copy(dest_ref, dest_smem, sem_idx)
    idx_copy.start()

    @pl.when(i == 0)
    def _():
        zero_ref[...] = jnp.zeros_like(zero_ref)

        def per_expert(e, carry):
            base = poff_ref[e]

            def start(rw, c):
                _row_copy(zero_ref, xs_hbm.at[_tile_rows(base + rw)], sem_rows).start()
                return c

            def wait(rw, c):
                _row_copy(zero_ref, xs_hbm.at[_tile_rows(base + rw)], sem_rows).wait()
                return c

            lax.fori_loop(cnt_ref[e], pcnt_ref[e], start, 0)
            lax.fori_loop(cnt_ref[e], pcnt_ref[e], wait, 0)
            return carry

        lax.fori_loop(0, N_EXPERTS, per_expert, 0)

    idx_copy.wait()

    def start(tk, c):
        for kk in range(TOP_K):
            _row_copy(xn_ref.at[_tile_rows(tk)], xs_hbm.at[_tile_rows(dest_smem[kk, tk])],
                      sem_rows).start(priority=kk % 2)
        return c

    lax.fori_loop(0, td, start, 0, unroll=8)
    for _ in range(TOP_K):
        _row_copy(xn_ref, xs_hbm.at[pl.ds(0, td * ROW_TILE)], sem_rows).wait()


def _dispatch(cnt, pcnt, poff, xn_tiles, route, n_rows):
    t = route.shape[1]
    td = DISPATCH_TILE
    poffc = jnp.broadcast_to(poff[:, None], (N_EXPERTS, LANES))
    return pl.pallas_call(
        _dispatch_kernel,
        grid_spec=pltpu.PrefetchScalarGridSpec(
            num_scalar_prefetch=3,
            grid=(t // td,),
            in_specs=[pl.BlockSpec((td * ROW_TILE, LANES), lambda i, *_: (i, 0)),
                      pl.BlockSpec((8, td), lambda i, *_: (0, i)),
                      pl.BlockSpec((N_EXPERTS, LANES), lambda i, *_: (0, 0))],
            out_specs=[pl.BlockSpec(memory_space=pl.ANY),
                       pl.BlockSpec((8, td), lambda i, *_: (0, i))],
            scratch_shapes=[pltpu.SMEM((8, td), I32),
                            pltpu.VMEM((ROW_TILE, LANES), F32),
                            pltpu.SemaphoreType.DMA,
                            pltpu.SemaphoreType.DMA]),
        out_shape=[jax.ShapeDtypeStruct((n_rows * ROW_TILE, LANES), F32),
                   jax.ShapeDtypeStruct((8, t), I32)],
        compiler_params=_cparams("arbitrary"),
        name="dispatch",
    )(cnt, pcnt, poff, xn_tiles, route, poffc)


DEINT = 256


def _expert_kernel(be_ref, nu_ref, xs_ref, w1_ref, b1g_ref, b1l_ref, w2_ref, b2_ref, sel_ref,
                   ys_ref, w1g_ref, w1l_ref, w2b_ref):
    i = pl.program_id(0)
    new_expert = jnp.logical_or(i == 0, be_ref[i] != be_ref[jnp.maximum(i - 1, 0)])

    @pl.when(jnp.logical_and(new_expert, i < nu_ref[0]))
    def _():
        half = DEINT // 2
        for c in range(2 * D_MODEL // DEINT):
            cols = w1_ref[0, :, DEINT * c:DEINT * (c + 1)].astype(BF16)
            sep = jnp.dot(cols, sel_ref[...], preferred_element_type=F32)
            w1g_ref[:, half * c:half * (c + 1)] = sep[:, :half].astype(BF16)
            w1l_ref[:, half * c:half * (c + 1)] = sep[:, half:].astype(BF16)
        w2b_ref[...] = w2_ref[0].astype(BF16)

    @pl.when(i < nu_ref[0])
    def _():
        x = _load_row_tiles(xs_ref, 0, MOE_ROWS).astype(BF16)
        hg = jnp.dot(x, w1g_ref[...], preferred_element_type=F32) + b1g_ref[0]
        hl = jnp.dot(x, w1l_ref[...], preferred_element_type=F32) + b1l_ref[0]
        x_glu = jnp.minimum(hg, SWIGLU_LIMIT)
        x_lin = jnp.clip(hl, -SWIGLU_LIMIT, SWIGLU_LIMIT)
        act = x_glu * jax.nn.sigmoid(SWIGLU_ALPHA * x_glu) * (x_lin + 1.0)
        _store_row_tiles(ys_ref, jnp.dot(act.astype(BF16), w2b_ref[...],
                                         preferred_element_type=F32) + b2_ref[0])

    @pl.when(i >= nu_ref[0])
    def _():
        ys_ref[...] = jnp.zeros_like(ys_ref)


def _experts(blk_exp, n_used, xs, w1, b1g, b1l, w2, b2):
    n_rows = xs.shape[0] // ROW_TILE
    tm = MOE_ROWS
    wmap = lambda i, be, nu: (be[i], 0, 0)
    j = jnp.arange(DEINT, dtype=I32)
    sel = (jnp.where(j % 2 == 0, j // 2, DEINT // 2 + j // 2)[:, None] == j[None, :]).astype(BF16)
    return pl.pallas_call(
        _expert_kernel,
        grid_spec=pltpu.PrefetchScalarGridSpec(
            num_scalar_prefetch=2,
            grid=(n_rows // tm,),
            in_specs=[pl.BlockSpec((tm * ROW_TILE, LANES),
                                   lambda i, be, nu: (jnp.minimum(i, nu[0] - 1), 0)),
                      pl.BlockSpec((1, D_MODEL, 2 * D_MODEL), wmap),
                      pl.BlockSpec((1, 1, D_MODEL), wmap),
                      pl.BlockSpec((1, 1, D_MODEL), wmap),
                      pl.BlockSpec((1, D_MODEL, D_MODEL), wmap),
                      pl.BlockSpec((1, 1, D_MODEL), wmap),
                      pl.BlockSpec((DEINT, DEINT), lambda i, be, nu: (0, 0))],
            out_specs=pl.BlockSpec((tm * ROW_TILE, LANES), lambda i, be, nu: (i, 0)),
            scratch_shapes=[pltpu.VMEM((D_MODEL, D_MODEL), BF16),
                            pltpu.VMEM((D_MODEL, D_MODEL), BF16),
                            pltpu.VMEM((D_MODEL, D_MODEL), BF16)]),
        out_shape=jax.ShapeDtypeStruct((n_rows * ROW_TILE, LANES), F32),
        compiler_params=pltpu.CompilerParams(dimension_semantics=("arbitrary",),
                                             vmem_limit_bytes=EXPERT_VMEM_LIMIT_BYTES),
        name="experts",
    )(blk_exp, n_used, xs, w1, b1g, b1l, w2, b2, sel)


def _combine_kernel(h_ref, gate_ref, nf_ref, dest_hbm, ys_hbm, out_ref,
                    dest_smem, buf_ref, sem_idx, sem_rows):
    i = pl.program_id(0)
    n = pl.num_programs(0)
    tc = h_ref.shape[0]
    slot_rows = TOP_K * tc * ROW_TILE

    def idx_copy(j):
        return pltpu.make_async_copy(dest_hbm.at[:, pl.ds(j * tc, tc)], dest_smem.at[j % 2],
                                     sem_idx.at[j % 2])

    def gather(j):
        s = j % 2

        def start(tk, c):
            for kk in range(TOP_K):
                _row_copy(ys_hbm.at[_tile_rows(dest_smem[s, kk, tk])],
                          buf_ref.at[_tile_rows((s * TOP_K + kk) * tc + tk)],
                          sem_rows.at[s]).start(priority=kk % 2)
            return c

        lax.fori_loop(0, tc, start, 0, unroll=8)

    @pl.when(i == 0)
    def _():
        idx_copy(i).start()
        idx_copy(i).wait()
        gather(i)

        @pl.when(n > 1)
        def _():
            idx_copy(i + 1).start()

    @pl.when(i + 1 < n)
    def _():
        idx_copy(i + 1).wait()
        gather(i + 1)

    @pl.when(i + 2 < n)
    def _():
        idx_copy(i + 2).start()

    cur = i % 2
    for kk in range(TOP_K):
        _row_copy(ys_hbm.at[pl.ds(0, tc * ROW_TILE)],
                  buf_ref.at[pl.ds(pl.multiple_of((cur * TOP_K + kk) * tc * ROW_TILE, ROW_TILE),
                                   tc * ROW_TILE)], sem_rows.at[cur]).wait()

    gate = gate_ref[...]
    acc = jnp.zeros(h_ref.shape, F32)
    for kk in range(TOP_K):
        acc = acc + gate[:, kk:kk + 1] * _load_row_tiles(
            buf_ref, cur * slot_rows + kk * tc * ROW_TILE, tc)
    h = h_ref[...] + acc
    out_ref[...] = h * lax.rsqrt(jnp.mean(h * h, axis=-1, keepdims=True) + RMS_EPS) * nf_ref[...]


def _combine(h, gates_tok, norm_f, dest, ys):
    t = h.shape[0]
    tc = COMBINE_TILE
    return pl.pallas_call(
        _combine_kernel,
        grid=(t // tc,),
        in_specs=[pl.BlockSpec((tc, D_MODEL), lambda i: (i, 0)),
                  pl.BlockSpec((tc, TOP_K), lambda i: (i, 0)),
                  pl.BlockSpec((1, D_MODEL), lambda i: (0, 0)),
                  pl.BlockSpec(memory_space=pl.ANY),
                  pl.BlockSpec(memory_space=pl.ANY)],
        out_specs=pl.BlockSpec((tc, D_MODEL), lambda i: (i, 0)),
        scratch_shapes=[pltpu.SMEM((2, 8, tc), I32),
                        pltpu.VMEM((2 * TOP_K * tc * ROW_TILE, LANES), F32),
                        pltpu.SemaphoreType.DMA((2,)),
                        pltpu.SemaphoreType.DMA((2,))],
        out_shape=jax.ShapeDtypeStruct((t, D_MODEL), F32),
        compiler_params=_cparams("arbitrary"),
        name="combine",
    )(h, gates_tok, norm_f, dest, ys)


def _moe_plan(counts, n_blocks):
    pcnt = ((counts + MOE_ROWS - 1) // MOE_ROWS) * MOE_ROWS
    pend = jnp.cumsum(pcnt)
    poff = pend - pcnt
    n_used = (pend[-1:] // MOE_ROWS).astype(I32)
    blk_start = jnp.arange(n_blocks, dtype=I32) * MOE_ROWS
    blk_exp = jnp.minimum(jnp.sum(pend[None, :] <= blk_start[:, None], axis=1),
                          N_EXPERTS - 1).astype(I32)
    return pcnt.astype(I32), poff.astype(I32), n_used, blk_exp


def kernel(x, norm1_g, w_in, s5_lambda_re, s5_lambda_im, s5_log_dt, s5_b_re, s5_b_im, s5_c_re, s5_c_im, s5_d, s5_w_glu, s5_b_glu, rw_mu, rw_w0, rw_w2, rw_a0, rw_a2, rw_g2, rw_k_k, rw_k_a, rw_r_k, rw_ln_g, rw_ln_b, rw_w_out, w_out, norm2_g, router_w, router_b, moe_w1, moe_b1, moe_w2, moe_b2, norm_f_g):
    bsz, seq, d = x.shape
    t = bsz * seq
    assert d == D_MODEL and norm1_g.shape[0] == 1
    assert seq % S5_STEPS == 0 and seq % RW_CHUNK == 0 and bsz % 8 == 0
    assert t % IN_TILE == 0 and t % MIX_TILE == 0 and t % ROUTE_TILE == 0
    assert t % DISPATCH_TILE == 0 and t % COMBINE_TILE == 0
    x2d = x.reshape(t, d)

    u, prw, gs5, grw = _in_proj(x2d, norm1_g, w_in[0].astype(BF16), bsz, seq)
    a_vec, bd, cd = _s5_params(s5_lambda_re[0], s5_lambda_im[0], s5_log_dt[0], s5_b_re[0],
                               s5_b_im[0], s5_c_re[0], s5_c_im[0])
    s5g = _s5_branch(u.reshape(bsz, seq, S5_WIDTH), gs5.reshape(bsz, seq, D_MODEL),
                     a_vec, bd, cd, s5_d, s5_w_glu[0].astype(BF16), s5_b_glu, bsz, seq)
    zpad = jnp.zeros((64, RW_WIDTH), F32)
    w2p = jnp.concatenate([rw_w2[0], zpad], axis=0).astype(BF16)
    a2p = jnp.concatenate([zpad, rw_a2[0]], axis=0).astype(BF16)
    rwg = _rwkv_branch(prw.reshape(bsz, seq, RW_COLS), grw.reshape(bsz, seq, D_MODEL), rw_mu,
                       rw_w0, w2p, rw_a0, a2p, rw_g2[0].astype(BF16), rw_k_k,
                       rw_k_a, rw_r_k.reshape(1, RW_WIDTH), rw_ln_g, rw_ln_b,
                       rw_w_out[0].astype(BF16), bsz, seq)
    lane_pad = ((0, 0), (0, LANES - N_EXPERTS))
    h, xn2, logits_t = _mix(x2d, s5g.reshape(t, D_MODEL), rwg.reshape(t, D_MODEL),
                            w_out[0].astype(BF16), norm2_g,
                            jnp.pad(router_w[0], lane_pad).astype(BF16),
                            jnp.pad(router_b, lane_pad))

    route, gates, cnt2d = _route(logits_t)
    counts = cnt2d[:, 0]
    n_blocks = (t * TOP_K) // MOE_ROWS + N_EXPERTS
    pcnt, poff, n_used, blk_exp = _moe_plan(counts, n_blocks)
    xs, dest = _dispatch(counts, pcnt, poff, xn2, route, n_blocks * MOE_ROWS)
    ys = _experts(blk_exp, n_used, xs, moe_w1[0],
                  moe_b1[0][:, None, 0::2], moe_b1[0][:, None, 1::2],
                  moe_w2[0], moe_b2[0][:, None, :])
    out = _combine(h, gates[:TOP_K].T, norm_f_g.reshape(1, D_MODEL), dest, ys)
    return out.reshape(bsz, seq, d)
```

```python
import functools
import math

import jax
import jax.numpy as jnp
from jax import lax
from jax.experimental import pallas as pl
from jax.experimental.pallas import tpu as pltpu

F32 = jnp.float32
BF16 = jnp.bfloat16
I32 = jnp.int32

D_MODEL = 1024
S5_WIDTH = 512
S5_GROUP = 16
S5_GROUPS = 32
S5_STATE = 64
S5_HALF_GROUPS = 16
S5_HALF_IN = S5_HALF_GROUPS * S5_GROUP
S5_HALF_ST = S5_HALF_GROUPS * S5_STATE
S5_ST_COLS = 4 * S5_HALF_ST
RW_WIDTH = 512
RW_HEAD = 64
RW_HEADS = 8
RW_COLS = 1792
IN_COLS = 4352
N_EXPERTS = 32
TOP_K = 4
SWIGLU_ALPHA = 1.702
SWIGLU_LIMIT = 7.0
RMS_EPS = 1e-5
GN_EPS = 64e-5
L2_EPS = 1e-12

VMEM_LIMIT_BYTES = 48 * 1024 * 1024
EXPERT_VMEM_LIMIT_BYTES = 56 * 1024 * 1024

IN_TILE = 256
MIX_TILE = 512
S5_STEPS = 32
S5_LANES = 512
RW_CHUNK = 64
RW_BATCH = 4
ROUTE_TILE = 512
MOE_ROWS = 512
DISPATCH_TILE = 512
COMBINE_TILE = 256


def _cparams(*sem):
    return pltpu.CompilerParams(dimension_semantics=sem, vmem_limit_bytes=VMEM_LIMIT_BYTES)


def _dot(a, b):
    return jnp.dot(a.astype(BF16), b.astype(BF16), preferred_element_type=F32)


def _dot_nt(a, b):
    return lax.dot_general(a.astype(BF16), b.astype(BF16), (((1,), (1,)), ((), ())),
                           preferred_element_type=F32)


def _dot_tn(a, b):
    return lax.dot_general(a.astype(BF16), b.astype(BF16), (((0,), (0,)), ((), ())),
                           preferred_element_type=F32)


def _dot_split(a, b_f32):
    hi = b_f32.astype(BF16)
    lo = (b_f32 - hi.astype(F32)).astype(BF16)
    return (jnp.dot(a, hi, preferred_element_type=F32)
            + jnp.dot(a, lo, preferred_element_type=F32))


def _split_dot(a_f32, b):
    hi = a_f32.astype(BF16)
    lo = (a_f32 - hi.astype(F32)).astype(BF16)
    return (jnp.dot(hi, b, preferred_element_type=F32)
            + jnp.dot(lo, b, preferred_element_type=F32))


def _in_proj_kernel(x_ref, g_ref, w_ref, u_ref, prw_ref, gs5_ref, grw_ref):
    x = x_ref[...]
    xn = x * lax.rsqrt(jnp.mean(x * x, axis=-1, keepdims=True) + RMS_EPS) * g_ref[...]
    p = _dot(xn, w_ref[...])
    c0, c1, c2 = S5_WIDTH, S5_WIDTH + RW_COLS, S5_WIDTH + RW_COLS + D_MODEL
    u_ref[...] = p[:, :c0].astype(BF16)
    prw_ref[...] = p[:, c0:c1]
    gs5_ref[...] = jax.nn.sigmoid(p[:, c1:c2]).astype(BF16)
    grw_ref[...] = jax.nn.sigmoid(p[:, c2:]).astype(BF16)


def _in_proj(x2d, norm_g, w_in_bf, bsz, seq):
    ts = IN_TILE
    t = bsz * seq
    row = lambda i: (i, 0)
    return pl.pallas_call(
        _in_proj_kernel,
        grid=(t // ts,),
        in_specs=[pl.BlockSpec((ts, D_MODEL), row),
                  pl.BlockSpec((1, D_MODEL), lambda i: (0, 0)),
                  pl.BlockSpec((D_MODEL, IN_COLS), lambda i: (0, 0))],
        out_specs=[pl.BlockSpec((ts, S5_WIDTH), row),
                   pl.BlockSpec((ts, RW_COLS), row),
                   pl.BlockSpec((ts, D_MODEL), row),
                   pl.BlockSpec((ts, D_MODEL), row)],
        out_shape=[jax.ShapeDtypeStruct((t, S5_WIDTH), BF16),
                   jax.ShapeDtypeStruct((t, RW_COLS), F32),
                   jax.ShapeDtypeStruct((t, D_MODEL), BF16),
                   jax.ShapeDtypeStruct((t, D_MODEL), BF16)],
        compiler_params=_cparams("parallel"),
        name="in_proj",
    )(x2d, norm_g, w_in_bf)


def _s5_kernel(bsz, u_ref, gs5_ref, perm_ref, a_ref, bd_ref, cd_ref, d_ref, wg_ref, bg_ref,
               out_ref, st_ref, xb_ref):
    @pl.when(pl.program_id(0) == 0)
    def _():
        st_ref[...] = jnp.zeros_like(st_ref)

    rows = bsz * S5_STEPS
    u = jnp.dot(perm_ref[0], u_ref[...].reshape(rows, S5_WIDTH),
                preferred_element_type=F32).astype(BF16)
    for h in range(2):
        xb_ref[:, 2 * S5_HALF_ST * h:2 * S5_HALF_ST * (h + 1)] = jnp.dot(
            u[:, S5_HALF_IN * h:S5_HALF_IN * (h + 1)], bd_ref[h], preferred_element_type=F32)

    for h in range(2):
        for j in range(S5_HALF_ST // S5_LANES):
            ro = 2 * S5_HALF_ST * h + S5_LANES * j
            io = ro + S5_HALF_ST
            ar = jnp.broadcast_to(a_ref[:, ro:ro + S5_LANES], (bsz, S5_LANES))
            ai = jnp.broadcast_to(a_ref[:, io:io + S5_LANES], (bsz, S5_LANES))

            def step(t, carry, ro=ro, io=io, ar=ar, ai=ai):
                xr, xi = carry
                r0 = pl.multiple_of(t * bsz, bsz)
                nr = ar * xr - ai * xi + xb_ref[pl.ds(r0, bsz), ro:ro + S5_LANES]
                ni = ar * xi + ai * xr + xb_ref[pl.ds(r0, bsz), io:io + S5_LANES]
                xb_ref[pl.ds(r0, bsz), ro:ro + S5_LANES] = nr
                xb_ref[pl.ds(r0, bsz), io:io + S5_LANES] = ni
                return nr, ni

            xr, xi = lax.fori_loop(0, S5_STEPS, step,
                                   (st_ref[:, ro:ro + S5_LANES], st_ref[:, io:io + S5_LANES]))
            st_ref[:, ro:ro + S5_LANES] = xr
            st_ref[:, io:io + S5_LANES] = xi

    ys = []
    for h in range(2):
        xh = xb_ref[:, 2 * S5_HALF_ST * h:2 * S5_HALF_ST * (h + 1)]
        ys.append(jnp.dot(xh.astype(BF16), cd_ref[h], preferred_element_type=F32))
    y = jnp.concatenate(ys, axis=1) + d_ref[...] * u.astype(F32)
    y = jax.nn.gelu(y)
    z = _dot(y, wg_ref[...]) + bg_ref[...]
    s5 = (z[:, :D_MODEL] * jax.nn.sigmoid(z[:, D_MODEL:])).astype(BF16)
    s5 = jnp.dot(perm_ref[1], s5, preferred_element_type=F32)
    gate = gs5_ref[...].reshape(rows, D_MODEL).astype(F32)
    out_ref[...] = (s5 * gate).astype(BF16).reshape(bsz, S5_STEPS, D_MODEL)


def _s5_params(lam_re, lam_im, log_dt, b_re, b_im, c_re, c_im):
    dt = jnp.exp(log_dt)[:, None]
    mag = jnp.exp(lam_re * dt)
    ab_re, ab_im = mag * jnp.cos(lam_im * dt), mag * jnp.sin(lam_im * dt)
    den = lam_re * lam_re + lam_im * lam_im
    nr, ni = ab_re - 1.0, ab_im
    f_re = (nr * lam_re + ni * lam_im) / den
    f_im = (ni * lam_re - nr * lam_im) / den
    bb_re = f_re[..., None] * b_re - f_im[..., None] * b_im
    bb_im = f_re[..., None] * b_im + f_im[..., None] * b_re
    eye = jnp.eye(S5_HALF_GROUPS, dtype=F32)

    def bd_half(bb):
        return jnp.einsum('gpc,gh->gchp', bb, eye).reshape(S5_HALF_IN, S5_HALF_ST)

    def cd_half(cc):
        return jnp.einsum('gcp,gh->gphc', cc, eye).reshape(S5_HALF_ST, S5_HALF_IN)

    bd, cd, a = [], [], []
    for h in range(2):
        sl = slice(S5_HALF_GROUPS * h, S5_HALF_GROUPS * (h + 1))
        bd.append(jnp.concatenate([bd_half(bb_re[sl]), bd_half(bb_im[sl])], axis=1))
        cd.append(jnp.concatenate([cd_half(c_re[sl]), cd_half(-c_im[sl])], axis=0))
        a += [ab_re[sl].reshape(1, S5_HALF_ST), ab_im[sl].reshape(1, S5_HALF_ST)]
    return (jnp.concatenate(a, axis=1), jnp.stack(bd).astype(BF16), jnp.stack(cd).astype(BF16))


def _s5_branch(u, gs5, a_vec, bd, cd, d_skip, w_glu_bf, b_glu, bsz, seq):
    rows = S5_STEPS * bsz
    const2 = lambda i: (0, 0)
    const3 = lambda i: (0, 0, 0)
    r_tb = jnp.arange(rows, dtype=I32)
    r_bt = (r_tb % bsz) * S5_STEPS + r_tb // bsz
    to_tb = (r_bt[:, None] == jnp.arange(rows, dtype=I32)[None, :]).astype(BF16)
    perm = jnp.stack([to_tb, to_tb.T])
    blk = lambda i: (0, i, 0)
    return pl.pallas_call(
        functools.partial(_s5_kernel, bsz),
        grid=(seq // S5_STEPS,),
        in_specs=[pl.BlockSpec((bsz, S5_STEPS, S5_WIDTH), blk),
                  pl.BlockSpec((bsz, S5_STEPS, D_MODEL), blk),
                  pl.BlockSpec((2, rows, rows), const3),
                  pl.BlockSpec((1, S5_ST_COLS), const2),
                  pl.BlockSpec((2, S5_HALF_IN, 2 * S5_HALF_ST), const3),
                  pl.BlockSpec((2, 2 * S5_HALF_ST, S5_HALF_IN), const3),
                  pl.BlockSpec((1, S5_WIDTH), const2),
                  pl.BlockSpec((S5_WIDTH, 2 * D_MODEL), const2),
                  pl.BlockSpec((1, 2 * D_MODEL), const2)],
        out_specs=pl.BlockSpec((bsz, S5_STEPS, D_MODEL), blk),
        out_shape=jax.ShapeDtypeStruct((bsz, seq, D_MODEL), BF16),
        scratch_shapes=[pltpu.VMEM((bsz, S5_ST_COLS), F32),
                        pltpu.VMEM((rows, S5_ST_COLS), F32)],
        compiler_params=_cparams("arbitrary"),
        name="s5_branch",
    )(u, gs5, perm, a_vec, bd, cd, d_skip, w_glu_bf, b_glu)


def _rwkv_kernel(p_ref, grw_ref, mu_ref, w0_ref, w2_ref, a0_ref, a2_ref, g2_ref, kk_ref,
                 ka_ref, rk_ref, lng_ref, lnb_ref, wo_ref, hsum_ref, tri_ref,
                 out_ref, st_ref, prev_ref):
    L, N, H, NB = RW_CHUNK, RW_HEAD, RW_HEADS, RW_BATCH
    rows = NB * L

    @pl.when(pl.program_id(1) == 0)
    def _():
        st_ref[...] = jnp.zeros_like(st_ref)
        prev_ref[...] = jnp.zeros_like(prev_ref)

    p = p_ref[...].reshape(rows, RW_COLS)
    rid = lax.broadcasted_iota(I32, (rows, 1), 0)
    prev = pltpu.roll(p, 1, axis=0)
    for b in range(NB):
        prev = jnp.where(rid == b * L, prev_ref[b:b + 1, :], prev)
        prev_ref[b:b + 1, :] = p[(b + 1) * L - 1:(b + 1) * L, :]
    xs = p + mu_ref[...] * (prev - p)
    c1, c2, c3 = RW_WIDTH, 2 * RW_WIDTH, 3 * RW_WIDTH
    r, k, v = xs[:, :c1], xs[:, c1:c2], xs[:, c2:c3]
    lora = xs[:, c3:c3 + 128]
    gd = xs[:, c3 + 128:]
    hsum = hsum_ref[...]

    w_log = -jax.nn.softplus(-(w0_ref[...] + _dot(jnp.tanh(lora), w2_ref[...]))) - 0.5
    e = jnp.exp(w_log)
    a = jax.nn.sigmoid(a0_ref[...] + _dot(lora, a2_ref[...]))
    g = _dot(jax.nn.sigmoid(gd), g2_ref[...])
    kk = k * kk_ref[...]
    kk = kk / jnp.maximum(jnp.sqrt(_split_dot(kk * kk, hsum)), L2_EPS)
    k = k * (1.0 + (a - 1.0) * ka_ref[...])
    avec, bvec = -kk, kk * a

    cum = _dot_split(tri_ref[...], e)
    cum_last = jnp.concatenate(
        [jnp.broadcast_to(cum[(b + 1) * L - 1:(b + 1) * L, :], (L, RW_WIDTH)) for b in range(NB)],
        axis=0)
    dec_in = jnp.exp(-cum)
    at = avec * jnp.exp(e - cum)
    rt = r * dec_in
    grow = jnp.exp(cum)
    bt, kt = bvec * grow, k * grow
    tail = jnp.exp(cum - cum_last)
    bw, kw = bvec * tail, k * tail
    w_all = jnp.exp(-cum_last)

    ti = lax.broadcasted_iota(I32, (L, 2 * L), 0)
    si = lax.broadcasted_iota(I32, (L, 2 * L), 1)
    lane = lax.broadcasted_iota(I32, (1, 2 * L), 1)
    left, right = (si < L), (si >= L)
    strict = jnp.where(right, si - L, si) < ti
    m_ab, m_ak = left & strict, right & strict
    m_incl = jnp.where(right, si - L, si) <= ti
    own = [jnp.where(lane < L, 1.0, 0.0), jnp.where(lane >= L, 1.0, 0.0)]
    zeros_nk = jnp.zeros((N, 2 * N), F32)

    chains = range(NB * H)
    par = [c % 2 for c in chains]
    rb = [slice(L * (c // H), L * (c // H + 1)) for c in chains]
    slab = [slice(2 * N * ((c % H) // 2), 2 * N * ((c % H) // 2 + 1)) for c in chains]
    s_old = [st_ref[c] for c in chains]
    vr = [v[rb[c], slab[c]] * own[1] if par[c] == 1 else
          pltpu.roll(v[rb[c], slab[c]] * own[0], N, axis=1) for c in chains]
    pa = []
    for c in chains:
        q, r_, sl = par[c], rb[c], slab[c]
        ar_ = jnp.concatenate([at[r_, sl], rt[r_, sl]], axis=0) * own[q]
        rhs = [bt[r_, sl], kt[r_, sl], zeros_nk, s_old[c]]
        pa.append(_dot_nt(ar_, jnp.concatenate(rhs, axis=0)))
    z = []
    for c in chains:
        top = pa[c][:L, :2 * L]
        z.append(pa[c][:L, 2 * L:] + jnp.where(m_ab, top, 0.0)
                 + _dot(jnp.where(m_ak, top, 0.0), jnp.concatenate([vr[c], vr[c]], axis=0)))
    for _ in range(6):
        z = [z[c] * own[1] + _dot(z[c][:, :L], z[c]) for c in chains]
    ys = []
    for c in chains:
        ys.append(pa[c][L:, 2 * L:] + _dot(jnp.where(m_incl, pa[c][L:, :2 * L], 0.0),
                                           jnp.concatenate([z[c], vr[c]], axis=0)))
    for c in chains:
        q, r_, sl = par[c], rb[c], slab[c]
        bkw = jnp.concatenate([bw[r_, sl], kw[r_, sl]], axis=0) * own[q]
        upd = _dot_tn(jnp.concatenate([z[c], vr[c]], axis=0), bkw)
        st_ref[c] = s_old[c] * w_all[L * (c // H):L * (c // H) + 1, sl] + upd[N:, :]
    y = jnp.concatenate(
        [jnp.concatenate([pltpu.roll(ys[b * H + h], N, axis=1) + ys[b * H + h + 1]
                          for h in range(0, H, 2)], axis=1)
         for b in range(NB)], axis=0)

    inv_n = 1.0 / N
    mean = _split_dot(y, hsum) * inv_n
    dlt = y - mean
    var = _split_dot(dlt * dlt, hsum) * inv_n
    y = dlt * lax.rsqrt(var + GN_EPS) * lng_ref[...] + lnb_ref[...]
    bonus = _split_dot(r * k * rk_ref[...], hsum) * v
    y = (y + bonus) * g
    gate = grw_ref[...].reshape(rows, D_MODEL).astype(F32)
    out_ref[...] = (_dot(y, wo_ref[...]) * gate).astype(BF16).reshape(NB, L, D_MODEL)


def _rwkv_branch(prw, grw, mu, w0, w2p, a0, a2p, g2, k_k, k_a, r_k, ln_g, ln_b, wo_bf, bsz, seq):
    L, nb = RW_CHUNK, RW_BATCH
    blk = lambda b, c: (b, c, 0)
    const = lambda b, c: (0, 0)
    hsum = jnp.kron(jnp.eye(RW_HEADS, dtype=F32), jnp.ones((RW_HEAD, RW_HEAD), F32)).astype(BF16)
    tri = jnp.kron(jnp.eye(nb, dtype=F32), jnp.tril(jnp.ones((L, L), F32))).astype(BF16)
    vec = lambda n: pl.BlockSpec((1, n), const)
    return pl.pallas_call(
        _rwkv_kernel,
        grid=(bsz // nb, seq // L),
        in_specs=[pl.BlockSpec((nb, L, RW_COLS), blk),
                  pl.BlockSpec((nb, L, D_MODEL), blk),
                  vec(RW_COLS), vec(RW_WIDTH),
                  pl.BlockSpec((128, RW_WIDTH), const),
                  vec(RW_WIDTH),
                  pl.BlockSpec((128, RW_WIDTH), const),
                  pl.BlockSpec((128, RW_WIDTH), const),
                  vec(RW_WIDTH), vec(RW_WIDTH), vec(RW_WIDTH), vec(RW_WIDTH), vec(RW_WIDTH),
                  pl.BlockSpec((RW_WIDTH, D_MODEL), const),
                  pl.BlockSpec((RW_WIDTH, RW_WIDTH), const),
                  pl.BlockSpec((nb * L, nb * L), const)],
        out_specs=pl.BlockSpec((nb, L, D_MODEL), blk),
        out_shape=jax.ShapeDtypeStruct((bsz, seq, D_MODEL), BF16),
        scratch_shapes=[pltpu.VMEM((nb * RW_HEADS, RW_HEAD, 2 * RW_HEAD), F32),
                        pltpu.VMEM((nb, RW_COLS), F32)],
        compiler_params=_cparams("arbitrary", "arbitrary"),
        name="rwkv_branch",
    )(prw, grw, mu, w0, w2p, a0, a2p, g2, k_k, k_a, r_k, ln_g, ln_b, wo_bf, hsum, tri)


ROW_TILE = 8
LANES = 128


def _store_row_tiles(ref, val):
    m = val.shape[0]
    for s in range(ROW_TILE):
        ref[pl.ds(s, m, stride=ROW_TILE), :] = val[:, LANES * s:LANES * (s + 1)]


def _load_row_tiles(ref, start, m):
    return jnp.concatenate([ref[pl.ds(start + s, m, stride=ROW_TILE), :]
                            for s in range(ROW_TILE)], axis=1)


def _mix_kernel(x_ref, s5_ref, rw_ref, wout_ref, g2_ref, rw_w_ref, rb_ref,
                h_ref, xn_ref, lg_ref):
    mixed = s5_ref[...].astype(F32) + rw_ref[...].astype(F32)
    h = x_ref[...] + _dot(mixed, wout_ref[...])
    h_ref[...] = h
    xn = h * lax.rsqrt(jnp.mean(h * h, axis=-1, keepdims=True) + RMS_EPS) * g2_ref[...]
    _store_row_tiles(xn_ref, xn)
    logits = _dot(xn, rw_w_ref[...]) + rb_ref[...]
    lg_ref[...] = logits.T[:N_EXPERTS, :]


def _mix(x2d, s5g, rwg, w_out_bf, norm2_g, router_w_pad, router_b_pad):
    ts = MIX_TILE
    t = x2d.shape[0]
    row = lambda i: (i, 0)
    const = lambda i: (0, 0)
    return pl.pallas_call(
        _mix_kernel,
        grid=(t // ts,),
        in_specs=[pl.BlockSpec((ts, D_MODEL), row),
                  pl.BlockSpec((ts, D_MODEL), row),
                  pl.BlockSpec((ts, D_MODEL), row),
                  pl.BlockSpec((D_MODEL, D_MODEL), const),
                  pl.BlockSpec((1, D_MODEL), const),
                  pl.BlockSpec((D_MODEL, LANES), const),
                  pl.BlockSpec((1, LANES), const)],
        out_specs=[pl.BlockSpec((ts, D_MODEL), row),
                   pl.BlockSpec((ts * ROW_TILE, LANES), row),
                   pl.BlockSpec((N_EXPERTS, ts), lambda i: (0, i))],
        out_shape=[jax.ShapeDtypeStruct((t, D_MODEL), F32),
                   jax.ShapeDtypeStruct((t * ROW_TILE, LANES), F32),
                   jax.ShapeDtypeStruct((N_EXPERTS, t), F32)],
        compiler_params=_cparams("parallel"),
        name="mix_out_proj",
    )(x2d, s5g, rwg, w_out_bf, norm2_g, router_w_pad, router_b_pad)


def _route_kernel(lg_ref, triu_ref, route_ref, gate_ref, cnt_ref, run_ref):
    @pl.when(pl.program_id(0) == 0)
    def _():
        run_ref[...] = jnp.zeros_like(run_ref)

    lg = lg_ref[...]
    tr = lg.shape[1]
    ie = lax.broadcasted_iota(I32, lg.shape, 0)
    sel = jnp.zeros(lg.shape, F32)
    hots, vals, idxs = [], [], []
    for _ in range(TOP_K):
        m = jnp.max(lg, axis=0, keepdims=True)
        idx = jnp.min(jnp.where(lg == m, ie, N_EXPERTS), axis=0, keepdims=True)
        hot = ie == idx
        hots.append(hot)
        vals.append(m)
        idxs.append(idx)
        sel = jnp.where(hot, 1.0, sel)
        lg = jnp.where(hot, -jnp.inf, lg)
    ex = [jnp.exp(vk - vals[0]) for vk in vals]
    den = ex[0] + ex[1] + ex[2] + ex[3]
    before = jnp.dot(sel.astype(BF16), triu_ref[...], preferred_element_type=F32) + run_ref[:, 0:1]
    ranks = [jnp.sum(jnp.where(hot, before, 0.0), axis=0, keepdims=True) for hot in hots]
    run = run_ref[...] + jnp.sum(sel, axis=1, keepdims=True)
    run_ref[...] = run
    cnt_ref[...] = run.astype(I32)
    route_ref[...] = jnp.concatenate(idxs + [rk.astype(I32) for rk in ranks], axis=0)
    gate_ref[...] = jnp.concatenate([ek / den for ek in ex] + [jnp.zeros((4, tr), F32)], axis=0)


def _route(logits_t):
    t = logits_t.shape[1]
    tr = ROUTE_TILE
    triu = jnp.triu(jnp.ones((tr, tr), F32), k=1).astype(BF16)
    return pl.pallas_call(
        _route_kernel,
        grid=(t // tr,),
        in_specs=[pl.BlockSpec((N_EXPERTS, tr), lambda i: (0, i)),
                  pl.BlockSpec((tr, tr), lambda i: (0, 0))],
        out_specs=[pl.BlockSpec((8, tr), lambda i: (0, i)),
                   pl.BlockSpec((8, tr), lambda i: (0, i)),
                   pl.BlockSpec((N_EXPERTS, 128), lambda i: (0, 0))],
        out_shape=[jax.ShapeDtypeStruct((8, t), I32),
                   jax.ShapeDtypeStruct((8, t), F32),
                   jax.ShapeDtypeStruct((N_EXPERTS, 128), I32)],
        scratch_shapes=[pltpu.VMEM((N_EXPERTS, 128), F32)],
        compiler_params=_cparams("arbitrary"),
        name="route",
    )(logits_t, triu)


def _row_copy(src, dst, sem):
    return pltpu.make_async_copy(src, dst, sem)


def _tile_rows(row):
    return pl.ds(pl.multiple_of(row * ROW_TILE, ROW_TILE), ROW_TILE)


def _dispatch_kernel(cnt_ref, pcnt_ref, poff_ref, xn_ref, route_ref, poffc_ref, xs_hbm, dest_ref,
                     dest_smem, zero_ref, sem_idx, sem_rows):
    i = pl.program_id(0)
    td = route_ref.shape[1]

    rt = route_ref[...]
    ie = lax.broadcasted_iota(I32, (N_EXPERTS, td), 0)
    poffc = poffc_ref[:, 0:1]
    rows = []
    for kk in range(TOP_K):
        base = jnp.sum(jnp.where(ie == rt[kk:kk + 1, :], poffc, 0), axis=0, keepdims=True)
        rows.append(base + rt[TOP_K + kk:TOP_K + kk + 1, :])
    dest_ref[...] = jnp.concatenate(rows + [jnp.zeros((8 - TOP_K, td), I32)], axis=0)
    idx_copy = pltpu.make_async_copy(dest_ref, dest_smem, sem_idx)
    idx_copy.start()

    @pl.when(i == 0)
    def _():
        zero_ref[...] = jnp.zeros_like(zero_ref)

        def per_expert(e, carry):
            base = poff_ref[e]

            def start(rw, c):
                _row_copy(zero_ref, xs_hbm.at[_tile_rows(base + rw)], sem_rows).start()
                return c

            def wait(rw, c):
                _row_copy(zero_ref, xs_hbm.at[_tile_rows(base + rw)], sem_rows).wait()
                return c

            lax.fori_loop(cnt_ref[e], pcnt_ref[e], start, 0)
            lax.fori_loop(cnt_ref[e], pcnt_ref[e], wait, 0)
            return carry

        lax.fori_loop(0, N_EXPERTS, per_expert, 0)

    idx_copy.wait()

    def start(tk, c):
        for kk in range(TOP_K):
            _row_copy(xn_ref.at[_tile_rows(tk)], xs_hbm.at[_tile_rows(dest_smem[kk, tk])],
                      sem_rows).start(priority=kk % 2)
        return c

    lax.fori_loop(0, td, start, 0, unroll=8)
    for _ in range(TOP_K):
        _row_copy(xn_ref, xs_hbm.at[pl.ds(0, td * ROW_TILE)], sem_rows).wait()


def _dispatch(cnt, pcnt, poff, xn_tiles, route, n_rows):
    t = route.shape[1]
    td = DISPATCH_TILE
    poffc = jnp.broadcast_to(poff[:, None], (N_EXPERTS, LANES))
    return pl.pallas_call(
        _dispatch_kernel,
        grid_spec=pltpu.PrefetchScalarGridSpec(
            num_scalar_prefetch=3,
            grid=(t // td,),
            in_specs=[pl.BlockSpec((td * ROW_TILE, LANES), lambda i, *_: (i, 0)),
                      pl.BlockSpec((8, td), lambda i, *_: (0, i)),
                      pl.BlockSpec((N_EXPERTS, LANES), lambda i, *_: (0, 0))],
            out_specs=[pl.BlockSpec(memory_space=pl.ANY),
                       pl.BlockSpec((8, td), lambda i, *_: (0, i))],
            scratch_shapes=[pltpu.SMEM((8, td), I32),
                            pltpu.VMEM((ROW_TILE, LANES), F32),
                            pltpu.SemaphoreType.DMA,
                            pltpu.SemaphoreType.DMA]),
        out_shape=[jax.ShapeDtypeStruct((n_rows * ROW_TILE, LANES), F32),
                   jax.ShapeDtypeStruct((8, t), I32)],
        compiler_params=_cparams("arbitrary"),
        name="dispatch",
    )(cnt, pcnt, poff, xn_tiles, route, poffc)


DEINT = 256


def _expert_kernel(be_ref, nu_ref, xs_ref, w1_ref, b1g_ref, b1l_ref, w2_ref, b2_ref, sel_ref,
                   ys_ref, w1g_ref, w1l_ref, w2b_ref):
    i = pl.program_id(0)
    new_expert = jnp.logical_or(i == 0, be_ref[i] != be_ref[jnp.maximum(i - 1, 0)])

    @pl.when(jnp.logical_and(new_expert, i < nu_ref[0]))
    def _():
        half = DEINT // 2
        for c in range(2 * D_MODEL // DEINT):
            cols = w1_ref[0, :, DEINT * c:DEINT * (c + 1)].astype(BF16)
            sep = jnp.dot(cols, sel_ref[...], preferred_element_type=F32)
            w1g_ref[:, half * c:half * (c + 1)] = sep[:, :half].astype(BF16)
            w1l_ref[:, half * c:half * (c + 1)] = sep[:, half:].astype(BF16)
        w2b_ref[...] = w2_ref[0].astype(BF16)

    @pl.when(i < nu_ref[0])
    def _():
        x = _load_row_tiles(xs_ref, 0, MOE_ROWS).astype(BF16)
        hg = jnp.dot(x, w1g_ref[...], preferred_element_type=F32) + b1g_ref[0]
        hl = jnp.dot(x, w1l_ref[...], preferred_element_type=F32) + b1l_ref[0]
        x_glu = jnp.minimum(hg, SWIGLU_LIMIT)
        x_lin = jnp.clip(hl, -SWIGLU_LIMIT, SWIGLU_LIMIT)
        act = x_glu * jax.nn.sigmoid(SWIGLU_ALPHA * x_glu) * (x_lin + 1.0)
        _store_row_tiles(ys_ref, jnp.dot(act.astype(BF16), w2b_ref[...],
                                         preferred_element_type=F32) + b2_ref[0])

    @pl.when(i >= nu_ref[0])
    def _():
        ys_ref[...] = jnp.zeros_like(ys_ref)


def _experts(blk_exp, n_used, xs, w1, b1g, b1l, w2, b2):
    n_rows = xs.shape[0] // ROW_TILE
    tm = MOE_ROWS
    wmap = lambda i, be, nu: (be[i], 0, 0)
    j = jnp.arange(DEINT, dtype=I32)
    sel = (jnp.where(j % 2 == 0, j // 2, DEINT // 2 + j // 2)[:, None] == j[None, :]).astype(BF16)
    return pl.pallas_call(
        _expert_kernel,
        grid_spec=pltpu.PrefetchScalarGridSpec(
            num_scalar_prefetch=2,
            grid=(n_rows // tm,),
            in_specs=[pl.BlockSpec((tm * ROW_TILE, LANES),
                                   lambda i, be, nu: (jnp.minimum(i, nu[0] - 1), 0)),
                      pl.BlockSpec((1, D_MODEL, 2 * D_MODEL), wmap),
                      pl.BlockSpec((1, 1, D_MODEL), wmap),
                      pl.BlockSpec((1, 1, D_MODEL), wmap),
                      pl.BlockSpec((1, D_MODEL, D_MODEL), wmap),
                      pl.BlockSpec((1, 1, D_MODEL), wmap),
                      pl.BlockSpec((DEINT, DEINT), lambda i, be, nu: (0, 0))],
            out_specs=pl.BlockSpec((tm * ROW_TILE, LANES), lambda i, be, nu: (i, 0)),
            scratch_shapes=[pltpu.VMEM((D_MODEL, D_MODEL), BF16),
                            pltpu.VMEM((D_MODEL, D_MODEL), BF16),
                            pltpu.VMEM((D_MODEL, D_MODEL), BF16)]),
        out_shape=jax.ShapeDtypeStruct((n_rows * ROW_TILE, LANES), F32),
        compiler_params=pltpu.CompilerParams(dimension_semantics=("arbitrary",),
                                             vmem_limit_bytes=EXPERT_VMEM_LIMIT_BYTES),
        name="experts",
    )(blk_exp, n_used, xs, w1, b1g, b1l, w2, b2, sel)


def _combine_kernel(h_ref, gate_ref, nf_ref, dest_hbm, ys_hbm, out_ref,
                    dest_smem, buf_ref, sem_idx, sem_rows):
    i = pl.program_id(0)
    n = pl.num_programs(0)
    tc = h_ref.shape[0]
    slot_rows = TOP_K * tc * ROW_TILE

    def idx_copy(j):
        return pltpu.make_async_copy(dest_hbm.at[:, pl.ds(j * tc, tc)], dest_smem.at[j % 2],
                                     sem_idx.at[j % 2])

    def gather(j):
        s = j % 2

        def start(tk, c):
            for kk in range(TOP_K):
                _row_copy(ys_hbm.at[_tile_rows(dest_smem[s, kk, tk])],
                          buf_ref.at[_tile_rows((s * TOP_K + kk) * tc + tk)],
                          sem_rows.at[s]).start(priority=kk % 2)
            return c

        lax.fori_loop(0, tc, start, 0, unroll=8)

    @pl.when(i == 0)
    def _():
        idx_copy(i).start()
        idx_copy(i).wait()
        gather(i)

        @pl.when(n > 1)
        def _():
            idx_copy(i + 1).start()

    @pl.when(i + 1 < n)
    def _():
        idx_copy(i + 1).wait()
        gather(i + 1)

    @pl.when(i + 2 < n)
    def _():
        idx_copy(i + 2).start()

    cur = i % 2
    for kk in range(TOP_K):
        _row_copy(ys_hbm.at[pl.ds(0, tc * ROW_TILE)],
                  buf_ref.at[pl.ds(pl.multiple_of((cur * TOP_K + kk) * tc * ROW_TILE, ROW_TILE),
                                   tc * ROW_TILE)], sem_rows.at[cur]).wait()

    gate = gate_ref[...]
    acc = jnp.zeros(h_ref.shape, F32)
    for kk in range(TOP_K):
        acc = acc + gate[:, kk:kk + 1] * _load_row_tiles(
            buf_ref, cur * slot_rows + kk * tc * ROW_TILE, tc)
    h = h_ref[...] + acc
    out_ref[...] = h * lax.rsqrt(jnp.mean(h * h, axis=-1, keepdims=True) + RMS_EPS) * nf_ref[...]


def _combine(h, gates_tok, norm_f, dest, ys):
    t = h.shape[0]
    tc = COMBINE_TILE
    return pl.pallas_call(
        _combine_kernel,
        grid=(t // tc,),
        in_specs=[pl.BlockSpec((tc, D_MODEL), lambda i: (i, 0)),
                  pl.BlockSpec((tc, TOP_K), lambda i: (i, 0)),
                  pl.BlockSpec((1, D_MODEL), lambda i: (0, 0)),
                  pl.BlockSpec(memory_space=pl.ANY),
                  pl.BlockSpec(memory_space=pl.ANY)],
        out_specs=pl.BlockSpec((tc, D_MODEL), lambda i: (i, 0)),
        scratch_shapes=[pltpu.SMEM((2, 8, tc), I32),
                        pltpu.VMEM((2 * TOP_K * tc * ROW_TILE, LANES), F32),
                        pltpu.SemaphoreType.DMA((2,)),
                        pltpu.SemaphoreType.DMA((2,))],
        out_shape=jax.ShapeDtypeStruct((t, D_MODEL), F32),
        compiler_params=_cparams("arbitrary"),
        name="combine",
    )(h, gates_tok, norm_f, dest, ys)


def _moe_plan(counts, n_blocks):
    pcnt = ((counts + MOE_ROWS - 1) // MOE_ROWS) * MOE_ROWS
    pend = jnp.cumsum(pcnt)
    poff = pend - pcnt
    n_used = (pend[-1:] // MOE_ROWS).astype(I32)
    blk_start = jnp.arange(n_blocks, dtype=I32) * MOE_ROWS
    blk_exp = jnp.minimum(jnp.sum(pend[None, :] <= blk_start[:, None], axis=1),
                          N_EXPERTS - 1).astype(I32)
    return pcnt.astype(I32), poff.astype(I32), n_used, blk_exp


def kernel(x, norm1_g, w_in, s5_lambda_re, s5_lambda_im, s5_log_dt, s5_b_re, s5_b_im, s5_c_re, s5_c_im, s5_d, s5_w_glu, s5_b_glu, rw_mu, rw_w0, rw_w2, rw_a0, rw_a2, rw_g2, rw_k_k, rw_k_a, rw_r_k, rw_ln_g, rw_ln_b, rw_w_out, w_out, norm2_g, router_w, router_b, moe_w1, moe_b1, moe_w2, moe_b2, norm_f_g):
    bsz, seq, d = x.shape
    t = bsz * seq
    assert d == D_MODEL and norm1_g.shape[0] == 1
    assert seq % S5_STEPS == 0 and seq % RW_CHUNK == 0 and bsz % 8 == 0
    assert t % IN_TILE == 0 and t % MIX_TILE == 0 and t % ROUTE_TILE == 0
    assert t % DISPATCH_TILE == 0 and t % COMBINE_TILE == 0
    x2d = x.reshape(t, d)

    u, prw, gs5, grw = _in_proj(x2d, norm1_g, w_in[0].astype(BF16), bsz, seq)
    a_vec, bd, cd = _s5_params(s5_lambda_re[0], s5_lambda_im[0], s5_log_dt[0], s5_b_re[0],
                               s5_b_im[0], s5_c_re[0], s5_c_im[0])
    s5g = _s5_branch(u.reshape(bsz, seq, S5_WIDTH), gs5.reshape(bsz, seq, D_MODEL),
                     a_vec, bd, cd, s5_d, s5_w_glu[0].astype(BF16), s5_b_glu, bsz, seq)
    zpad = jnp.zeros((64, RW_WIDTH), F32)
    w2p = jnp.concatenate([rw_w2[0], zpad], axis=0).astype(BF16)
    a2p = jnp.concatenate([zpad, rw_a2[0]], axis=0).astype(BF16)
    rwg = _rwkv_branch(prw.reshape(bsz, seq, RW_COLS), grw.reshape(bsz, seq, D_MODEL), rw_mu,
                       rw_w0, w2p, rw_a0, a2p, rw_g2[0].astype(BF16), rw_k_k,
                       rw_k_a, rw_r_k.reshape(1, RW_WIDTH), rw_ln_g, rw_ln_b,
                       rw_w_out[0].astype(BF16), bsz, seq)
    lane_pad = ((0, 0), (0, LANES - N_EXPERTS))
    h, xn2, logits_t = _mix(x2d, s5g.reshape(t, D_MODEL), rwg.reshape(t, D_MODEL),
                            w_out[0].astype(BF16), norm2_g,
                            jnp.pad(router_w[0], lane_pad).astype(BF16),
                            jnp.pad(router_b, lane_pad))

    route, gates, cnt2d = _route(logits_t)
    counts = cnt2d[:, 0]
    n_blocks = (t * TOP_K) // MOE_ROWS + N_EXPERTS
    pcnt, poff, n_used, blk_exp = _moe_plan(counts, n_blocks)
    xs, dest = _dispatch(counts, pcnt, poff, xn2, route, n_blocks * MOE_ROWS)
    ys = _experts(blk_exp, n_used, xs, moe_w1[0],
                  moe_b1[0][:, None, 0::2], moe_b1[0][:, None, 1::2],
                  moe_w2[0], moe_b2[0][:, None, :])
    out = _combine(h, gates[:TOP_K].T, norm_f_g.reshape(1, D_MODEL), dest, ys)
    return out.reshape(bsz, seq, d)
```

```python
import functools
import math

import jax
import jax.numpy as jnp
from jax import lax
from jax.experimental import pallas as pl
from jax.experimental.pallas import tpu as pltpu

F32 = jnp.float32
BF16 = jnp.bfloat16
I32 = jnp.int32

D_MODEL = 1024
S5_WIDTH = 512
S5_GROUP = 16
S5_GROUPS = 32
S5_STATE = 64
S5_HALF_GROUPS = 16
S5_HALF_IN = S5_HALF_GROUPS * S5_GROUP
S5_HALF_ST = S5_HALF_GROUPS * S5_STATE
S5_ST_COLS = 4 * S5_HALF_ST
RW_WIDTH = 512
RW_HEAD = 64
RW_HEADS = 8
RW_COLS = 1792
IN_COLS = 4352
N_EXPERTS = 32
TOP_K = 4
SWIGLU_ALPHA = 1.702
SWIGLU_LIMIT = 7.0
RMS_EPS = 1e-5
GN_EPS = 64e-5
L2_EPS = 1e-12

VMEM_LIMIT_BYTES = 48 * 1024 * 1024
EXPERT_VMEM_LIMIT_BYTES = 56 * 1024 * 1024

IN_TILE = 256
MIX_TILE = 512
S5_STEPS = 32
S5_LANES = 512
RW_CHUNK = 64
RW_BATCH = 4
ROUTE_TILE = 512
MOE_ROWS = 512
DISPATCH_TILE = 512
COMBINE_TILE = 256


def _cparams(*sem):
    return pltpu.CompilerParams(dimension_semantics=sem, vmem_limit_bytes=VMEM_LIMIT_BYTES)


def _dot(a, b):
    return jnp.dot(a.astype(BF16), b.astype(BF16), preferred_element_type=F32)


def _dot_nt(a, b):
    return lax.dot_general(a.astype(BF16), b.astype(BF16), (((1,), (1,)), ((), ())),
                           preferred_element_type=F32)


def _dot_tn(a, b):
    return lax.dot_general(a.astype(BF16), b.astype(BF16), (((0,), (0,)), ((), ())),
                           preferred_element_type=F32)


def _dot_split(a, b_f32):
    hi = b_f32.astype(BF16)
    lo = (b_f32 - hi.astype(F32)).astype(BF16)
    return (jnp.dot(a, hi, preferred_element_type=F32)
            + jnp.dot(a, lo, preferred_element_type=F32))


def _split_dot(a_f32, b):
    hi = a_f32.astype(BF16)
    lo = (a_f32 - hi.astype(F32)).astype(BF16)
    return (jnp.dot(hi, b, preferred_element_type=F32)
            + jnp.dot(lo, b, preferred_element_type=F32))


def _in_proj_kernel(x_ref, g_ref, w_ref, u_ref, prw_ref, gs5_ref, grw_ref):
    x = x_ref[...]
    xn = x * lax.rsqrt(jnp.mean(x * x, axis=-1, keepdims=True) + RMS_EPS) * g_ref[...]
    p = _dot(xn, w_ref[...])
    c0, c1, c2 = S5_WIDTH, S5_WIDTH + RW_COLS, S5_WIDTH + RW_COLS + D_MODEL
    u_ref[...] = p[:, :c0].astype(BF16)
    prw_ref[...] = p[:, c0:c1]
    gs5_ref[...] = jax.nn.sigmoid(p[:, c1:c2]).astype(BF16)
    grw_ref[...] = jax.nn.sigmoid(p[:, c2:]).astype(BF16)


def _in_proj(x2d, norm_g, w_in_bf, bsz, seq):
    ts = IN_TILE
    t = bsz * seq
    row = lambda i: (i, 0)
    return pl.pallas_call(
        _in_proj_kernel,
        grid=(t // ts,),
        in_specs=[pl.BlockSpec((ts, D_MODEL), row),
                  pl.BlockSpec((1, D_MODEL), lambda i: (0, 0)),
                  pl.BlockSpec((D_MODEL, IN_COLS), lambda i: (0, 0))],
        out_specs=[pl.BlockSpec((ts, S5_WIDTH), row),
                   pl.BlockSpec((ts, RW_COLS), row),
                   pl.BlockSpec((ts, D_MODEL), row),
                   pl.BlockSpec((ts, D_MODEL), row)],
        out_shape=[jax.ShapeDtypeStruct((t, S5_WIDTH), BF16),
                   jax.ShapeDtypeStruct((t, RW_COLS), F32),
                   jax.ShapeDtypeStruct((t, D_MODEL), BF16),
                   jax.ShapeDtypeStruct((t, D_MODEL), BF16)],
        compiler_params=_cparams("parallel"),
        name="in_proj",
    )(x2d, norm_g, w_in_bf)


def _s5_kernel(bsz, u_ref, gs5_ref, perm_ref, a_ref, bd_ref, cd_ref, d_ref, wg_ref, bg_ref,
               out_ref, st_ref, xb_ref):
    @pl.when(pl.program_id(0) == 0)
    def _():
        st_ref[...] = jnp.zeros_like(st_ref)

    rows = bsz * S5_STEPS
    u = jnp.dot(perm_ref[0], u_ref[...].reshape(rows, S5_WIDTH),
                preferred_element_type=F32).astype(BF16)
    for h in range(2):
        xb_ref[:, 2 * S5_HALF_ST * h:2 * S5_HALF_ST * (h + 1)] = jnp.dot(
            u[:, S5_HALF_IN * h:S5_HALF_IN * (h + 1)], bd_ref[h], preferred_element_type=F32)

    for h in range(2):
        for j in range(S5_HALF_ST // S5_LANES):
            ro = 2 * S5_HALF_ST * h + S5_LANES * j
            io = ro + S5_HALF_ST
            ar = jnp.broadcast_to(a_ref[:, ro:ro + S5_LANES], (bsz, S5_LANES))
            ai = jnp.broadcast_to(a_ref[:, io:io + S5_LANES], (bsz, S5_LANES))

            def step(t, carry, ro=ro, io=io, ar=ar, ai=ai):
                xr, xi = carry
                r0 = pl.multiple_of(t * bsz, bsz)
                nr = ar * xr - ai * xi + xb_ref[pl.ds(r0, bsz), ro:ro + S5_LANES]
                ni = ar * xi + ai * xr + xb_ref[pl.ds(r0, bsz), io:io + S5_LANES]
                xb_ref[pl.ds(r0, bsz), ro:ro + S5_LANES] = nr
                xb_ref[pl.ds(r0, bsz), io:io + S5_LANES] = ni
                return nr, ni

            xr, xi = lax.fori_loop(0, S5_STEPS, step,
                                   (st_ref[:, ro:ro + S5_LANES], st_ref[:, io:io + S5_LANES]))
            st_ref[:, ro:ro + S5_LANES] = xr
            st_ref[:, io:io + S5_LANES] = xi

    ys = []
    for h in range(2):
        xh = xb_ref[:, 2 * S5_HALF_ST * h:2 * S5_HALF_ST * (h + 1)]
        ys.append(jnp.dot(xh.astype(BF16), cd_ref[h], preferred_element_type=F32))
    y = jnp.concatenate(ys, axis=1) + d_ref[...] * u.astype(F32)
    y = jax.nn.gelu(y)
    z = _dot(y, wg_ref[...]) + bg_ref[...]
    s5 = (z[:, :D_MODEL] * jax.nn.sigmoid(z[:, D_MODEL:])).astype(BF16)
    s5 = jnp.dot(perm_ref[1], s5, preferred_element_type=F32)
    gate = gs5_ref[...].reshape(rows, D_MODEL).astype(F32)
    out_ref[...] = (s5 * gate).astype(BF16).reshape(bsz, S5_STEPS, D_MODEL)


def _s5_params(lam_re, lam_im, log_dt, b_re, b_im, c_re, c_im):
    dt = jnp.exp(log_dt)[:, None]
    mag = jnp.exp(lam_re * dt)
    ab_re, ab_im = mag * jnp.cos(lam_im * dt), mag * jnp.sin(lam_im * dt)
    den = lam_re * lam_re + lam_im * lam_im
    nr, ni = ab_re - 1.0, ab_im
    f_re = (nr * lam_re + ni * lam_im) / den
    f_im = (ni * lam_re - nr * lam_im) / den
    bb_re = f_re[..., None] * b_re - f_im[..., None] * b_im
    bb_im = f_re[..., None] * b_im + f_im[..., None] * b_re
    eye = jnp.eye(S5_HALF_GROUPS, dtype=F32)

    def bd_half(bb):
        return jnp.einsum('gpc,gh->gchp', bb, eye).reshape(S5_HALF_IN, S5_HALF_ST)

    def cd_half(cc):
        return jnp.einsum('gcp,gh->gphc', cc, eye).reshape(S5_HALF_ST, S5_HALF_IN)

    bd, cd, a = [], [], []
    for h in range(2):
        sl = slice(S5_HALF_GROUPS * h, S5_HALF_GROUPS * (h + 1))
        bd.append(jnp.concatenate([bd_half(bb_re[sl]), bd_half(bb_im[sl])], axis=1))
        cd.append(jnp.concatenate([cd_half(c_re[sl]), cd_half(-c_im[sl])], axis=0))
        a += [ab_re[sl].reshape(1, S5_HALF_ST), ab_im[sl].reshape(1, S5_HALF_ST)]
    return (jnp.concatenate(a, axis=1), jnp.stack(bd).astype(BF16), jnp.stack(cd).astype(BF16))


def _s5_branch(u, gs5, a_vec, bd, cd, d_skip, w_glu_bf, b_glu, bsz, seq):
    rows = S5_STEPS * bsz
    const2 = lambda i: (0, 0)
    const3 = lambda i: (0, 0, 0)
    r_tb = jnp.arange(rows, dtype=I32)
    r_bt = (r_tb % bsz) * S5_STEPS + r_tb // bsz
    to_tb = (r_bt[:, None] == jnp.arange(rows, dtype=I32)[None, :]).astype(BF16)
    perm = jnp.stack([to_tb, to_tb.T])
    blk = lambda i: (0, i, 0)
    return pl.pallas_call(
        functools.partial(_s5_kernel, bsz),
        grid=(seq // S5_STEPS,),
        in_specs=[pl.BlockSpec((bsz, S5_STEPS, S5_WIDTH), blk),
                  pl.BlockSpec((bsz, S5_STEPS, D_MODEL), blk),
                  pl.BlockSpec((2, rows, rows), const3),
                  pl.BlockSpec((1, S5_ST_COLS), const2),
                  pl.BlockSpec((2, S5_HALF_IN, 2 * S5_HALF_ST), const3),
                  pl.BlockSpec((2, 2 * S5_HALF_ST, S5_HALF_IN), const3),
                  pl.BlockSpec((1, S5_WIDTH), const2),
                  pl.BlockSpec((S5_WIDTH, 2 * D_MODEL), const2),
                  pl.BlockSpec((1, 2 * D_MODEL), const2)],
        out_specs=pl.BlockSpec((bsz, S5_STEPS, D_MODEL), blk),
        out_shape=jax.ShapeDtypeStruct((bsz, seq, D_MODEL), BF16),
        scratch_shapes=[pltpu.VMEM((bsz, S5_ST_COLS), F32),
                        pltpu.VMEM((rows, S5_ST_COLS), F32)],
        compiler_params=_cparams("arbitrary"),
        name="s5_branch",
    )(u, gs5, perm, a_vec, bd, cd, d_skip, w_glu_bf, b_glu)


def _rwkv_kernel(p_ref, grw_ref, mu_ref, w0_ref, w2_ref, a0_ref, a2_ref, g2_ref, kk_ref,
                 ka_ref, rk_ref, lng_ref, lnb_ref, wo_ref, hsum_ref, tri_ref,
                 out_ref, st_ref, prev_ref):
    L, N, H, NB = RW_CHUNK, RW_HEAD, RW_HEADS, RW_BATCH
    rows = NB * L

    @pl.when(pl.program_id(1) == 0)
    def _():
        st_ref[...] = jnp.zeros_like(st_ref)
        prev_ref[...] = jnp.zeros_like(prev_ref)

    p = p_ref[...].reshape(rows, RW_COLS)
    rid = lax.broadcasted_iota(I32, (rows, 1), 0)
    prev = pltpu.roll(p, 1, axis=0)
    for b in range(NB):
        prev = jnp.where(rid == b * L, prev_ref[b:b + 1, :], prev)
        prev_ref[b:b + 1, :] = p[(b + 1) * L - 1:(b + 1) * L, :]
    xs = p + mu_ref[...] * (prev - p)
    c1, c2, c3 = RW_WIDTH, 2 * RW_WIDTH, 3 * RW_WIDTH
    r, k, v = xs[:, :c1], xs[:, c1:c2], xs[:, c2:c3]
    lora = xs[:, c3:c3 + 128]
    gd = xs[:, c3 + 128:]
    hsum = hsum_ref[...]

    w_log = -jax.nn.softplus(-(w0_ref[...] + _dot(jnp.tanh(lora), w2_ref[...]))) - 0.5
    e = jnp.exp(w_log)
    a = jax.nn.sigmoid(a0_ref[...] + _dot(lora, a2_ref[...]))
    g = _dot(jax.nn.sigmoid(gd), g2_ref[...])
    kk = k * kk_ref[...]
    kk = kk / jnp.maximum(jnp.sqrt(_split_dot(kk * kk, hsum)), L2_EPS)
    k = k * (1.0 + (a - 1.0) * ka_ref[...])
    avec, bvec = -kk, kk * a

    cum = _dot_split(tri_ref[...], e)
    cum_last = jnp.concatenate(
        [jnp.broadcast_to(cum[(b + 1) * L - 1:(b + 1) * L, :], (L, RW_WIDTH)) for b in range(NB)],
        axis=0)
    dec_in = jnp.exp(-cum)
    at = avec * jnp.exp(e - cum)
    rt = r * dec_in
    grow = jnp.exp(cum)
    bt, kt = bvec * grow, k * grow
    tail = jnp.exp(cum - cum_last)
    bw, kw = bvec * tail, k * tail
    w_all = jnp.exp(-cum_last)

    ti = lax.broadcasted_iota(I32, (L, 2 * L), 0)
    si = lax.broadcasted_iota(I32, (L, 2 * L), 1)
    lane = lax.broadcasted_iota(I32, (1, 2 * L), 1)
    left, right = (si < L), (si >= L)
    strict = jnp.where(right, si - L, si) < ti
    m_ab, m_ak = left & strict, right & strict
    m_incl = jnp.where(right, si - L, si) <= ti
    own = [jnp.where(lane < L, 1.0, 0.0), jnp.where(lane >= L, 1.0, 0.0)]
    zeros_nk = jnp.zeros((N, 2 * N), F32)

    chains = range(NB * H)
    par = [c % 2 for c in chains]
    rb = [slice(L * (c // H), L * (c // H + 1)) for c in chains]
    slab = [slice(2 * N * ((c % H) // 2), 2 * N * ((c % H) // 2 + 1)) for c in chains]
    s_old = [st_ref[c] for c in chains]
    vr = [v[rb[c], slab[c]] * own[1] if par[c] == 1 else
          pltpu.roll(v[rb[c], slab[c]] * own[0], N, axis=1) for c in chains]
    pa = []
    for c in chains:
        q, r_, sl = par[c], rb[c], slab[c]
        ar_ = jnp.concatenate([at[r_, sl], rt[r_, sl]], axis=0) * own[q]
        rhs = [bt[r_, sl], kt[r_, sl], zeros_nk, s_old[c]]
        pa.append(_dot_nt(ar_, jnp.concatenate(rhs, axis=0)))
    z = []
    for c in chains:
        top = pa[c][:L, :2 * L]
        z.append(pa[c][:L, 2 * L:] + jnp.where(m_ab, top, 0.0)
                 + _dot(jnp.where(m_ak, top, 0.0), jnp.concatenate([vr[c], vr[c]], axis=0)))
    for _ in range(6):
        z = [z[c] * own[1] + _dot(z[c][:, :L], z[c]) for c in chains]
    ys = []
    for c in chains:
        ys.append(pa[c][L:, 2 * L:] + _dot(jnp.where(m_incl, pa[c][L:, :2 * L], 0.0),
                                           jnp.concatenate([z[c], vr[c]], axis=0)))
    for c in chains:
        q, r_, sl = par[c], rb[c], slab[c]
        bkw = jnp.concatenate([bw[r_, sl], kw[r_, sl]], axis=0) * own[q]
        upd = _dot_tn(jnp.concatenate([z[c], vr[c]], axis=0), bkw)
        st_ref[c] = s_old[c] * w_all[L * (c // H):L * (c // H) + 1, sl] + upd[N:, :]
    y = jnp.concatenate(
        [jnp.concatenate([pltpu.roll(ys[b * H + h], N, axis=1) + ys[b * H + h + 1]
                          for h in range(0, H, 2)], axis=1)
         for b in range(NB)], axis=0)

    inv_n = 1.0 / N
    mean = _split_dot(y, hsum) * inv_n
    dlt = y - mean
    var = _split_dot(dlt * dlt, hsum) * inv_n
    y = dlt * lax.rsqrt(var + GN_EPS) * lng_ref[...] + lnb_ref[...]
    bonus = _split_dot(r * k * rk_ref[...], hsum) * v
    y = (y + bonus) * g
    gate = grw_ref[...].reshape(rows, D_MODEL).astype(F32)
    out_ref[...] = (_dot(y, wo_ref[...]) * gate).astype(BF16).reshape(NB, L, D_MODEL)


def _rwkv_branch(prw, grw, mu, w0, w2p, a0, a2p, g2, k_k, k_a, r_k, ln_g, ln_b, wo_bf, bsz, seq):
    L, nb = RW_CHUNK, RW_BATCH
    blk = lambda b, c: (b, c, 0)
    const = lambda b, c: (0, 0)
    hsum = jnp.kron(jnp.eye(RW_HEADS, dtype=F32), jnp.ones((RW_HEAD, RW_HEAD), F32)).astype(BF16)
    tri = jnp.kron(jnp.eye(nb, dtype=F32), jnp.tril(jnp.ones((L, L), F32))).astype(BF16)
    vec = lambda n: pl.BlockSpec((1, n), const)
    return pl.pallas_call(
        _rwkv_kernel,
        grid=(bsz // nb, seq // L),
        in_specs=[pl.BlockSpec((nb, L, RW_COLS), blk),
                  pl.BlockSpec((nb, L, D_MODEL), blk),
                  vec(RW_COLS), vec(RW_WIDTH),
                  pl.BlockSpec((128, RW_WIDTH), const),
                  vec(RW_WIDTH),
                  pl.BlockSpec((128, RW_WIDTH), const),
                  pl.BlockSpec((128, RW_WIDTH), const),
                  vec(RW_WIDTH), vec(RW_WIDTH), vec(RW_WIDTH), vec(RW_WIDTH), vec(RW_WIDTH),
                  pl.BlockSpec((RW_WIDTH, D_MODEL), const),
                  pl.BlockSpec((RW_WIDTH, RW_WIDTH), const),
                  pl.BlockSpec((nb * L, nb * L), const)],
        out_specs=pl.BlockSpec((nb, L, D_MODEL), blk),
        out_shape=jax.ShapeDtypeStruct((bsz, seq, D_MODEL), BF16),
        scratch_shapes=[pltpu.VMEM((nb * RW_HEADS, RW_HEAD, 2 * RW_HEAD), F32),
                        pltpu.VMEM((nb, RW_COLS), F32)],
        compiler_params=_cparams("arbitrary", "arbitrary"),
        name="rwkv_branch",
    )(prw, grw, mu, w0, w2p, a0, a2p, g2, k_k, k_a, r_k, ln_g, ln_b, wo_bf, hsum, tri)


ROW_TILE = 8
LANES = 128


def _store_row_tiles(ref, val):
    m = val.shape[0]
    for s in range(ROW_TILE):
        ref[pl.ds(s, m, stride=ROW_TILE), :] = val[:, LANES * s:LANES * (s + 1)]


def _load_row_tiles(ref, start, m):
    return jnp.concatenate([ref[pl.ds(start + s, m, stride=ROW_TILE), :]
                            for s in range(ROW_TILE)], axis=1)


def _mix_kernel(x_ref, s5_ref, rw_ref, wout_ref, g2_ref, rw_w_ref, rb_ref,
                h_ref, xn_ref, lg_ref):
    mixed = s5_ref[...].astype(F32) + rw_ref[...].astype(F32)
    h = x_ref[...] + _dot(mixed, wout_ref[...])
    h_ref[...] = h
    xn = h * lax.rsqrt(jnp.mean(h * h, axis=-1, keepdims=True) + RMS_EPS) * g2_ref[...]
    _store_row_tiles(xn_ref, xn)
    logits = _dot(xn, rw_w_ref[...]) + rb_ref[...]
    lg_ref[...] = logits.T[:N_EXPERTS, :]


def _mix(x2d, s5g, rwg, w_out_bf, norm2_g, router_w_pad, router_b_pad):
    ts = MIX_TILE
    t = x2d.shape[0]
    row = lambda i: (i, 0)
    const = lambda i: (0, 0)
    return pl.pallas_call(
        _mix_kernel,
        grid=(t // ts,),
        in_specs=[pl.BlockSpec((ts, D_MODEL), row),
                  pl.BlockSpec((ts, D_MODEL), row),
                  pl.BlockSpec((ts, D_MODEL), row),
                  pl.BlockSpec((D_MODEL, D_MODEL), const),
                  pl.BlockSpec((1, D_MODEL), const),
                  pl.BlockSpec((D_MODEL, LANES), const),
                  pl.BlockSpec((1, LANES), const)],
        out_specs=[pl.BlockSpec((ts, D_MODEL), row),
                   pl.BlockSpec((ts * ROW_TILE, LANES), row),
                   pl.BlockSpec((N_EXPERTS, ts), lambda i: (0, i))],
        out_shape=[jax.ShapeDtypeStruct((t, D_MODEL), F32),
                   jax.ShapeDtypeStruct((t * ROW_TILE, LANES), F32),
                   jax.ShapeDtypeStruct((N_EXPERTS, t), F32)],
        compiler_params=_cparams("parallel"),
        name="mix_out_proj",
    )(x2d, s5g, rwg, w_out_bf, norm2_g, router_w_pad, router_b_pad)


def _route_kernel(lg_ref, triu_ref, route_ref, gate_ref, cnt_ref, run_ref):
    @pl.when(pl.program_id(0) == 0)
    def _():
        run_ref[...] = jnp.zeros_like(run_ref)

    lg = lg_ref[...]
    tr = lg.shape[1]
    ie = lax.broadcasted_iota(I32, lg.shape, 0)
    sel = jnp.zeros(lg.shape, F32)
    hots, vals, idxs = [], [], []
    for _ in range(TOP_K):
        m = jnp.max(lg, axis=0, keepdims=True)
        idx = jnp.min(jnp.where(lg == m, ie, N_EXPERTS), axis=0, keepdims=True)
        hot = ie == idx
        hots.append(hot)
        vals.append(m)
        idxs.append(idx)
        sel = jnp.where(hot, 1.0, sel)
        lg = jnp.where(hot, -jnp.inf, lg)
    ex = [jnp.exp(vk - vals[0]) for vk in vals]
    den = ex[0] + ex[1] + ex[2] + ex[3]
    before = jnp.dot(sel.astype(BF16), triu_ref[...], preferred_element_type=F32) + run_ref[:, 0:1]
    ranks = [jnp.sum(jnp.where(hot, before, 0.0), axis=0, keepdims=True) for hot in hots]
    run = run_ref[...] + jnp.sum(sel, axis=1, keepdims=True)
    run_ref[...] = run
    cnt_ref[...] = run.astype(I32)
    route_ref[...] = jnp.concatenate(idxs + [rk.astype(I32) for rk in ranks], axis=0)
    gate_ref[...] = jnp.concatenate([ek / den for ek in ex] + [jnp.zeros((4, tr), F32)], axis=0)


def _route(logits_t):
    t = logits_t.shape[1]
    tr = ROUTE_TILE
    triu = jnp.triu(jnp.ones((tr, tr), F32), k=1).astype(BF16)
    return pl.pallas_call(
        _route_kernel,
        grid=(t // tr,),
        in_specs=[pl.BlockSpec((N_EXPERTS, tr), lambda i: (0, i)),
                  pl.BlockSpec((tr, tr), lambda i: (0, 0))],
        out_specs=[pl.BlockSpec((8, tr), lambda i: (0, i)),
                   pl.BlockSpec((8, tr), lambda i: (0, i)),
                   pl.BlockSpec((N_EXPERTS, 128), lambda i: (0, 0))],
        out_shape=[jax.ShapeDtypeStruct((8, t), I32),
                   jax.ShapeDtypeStruct((8, t), F32),
                   jax.ShapeDtypeStruct((N_EXPERTS, 128), I32)],
        scratch_shapes=[pltpu.VMEM((N_EXPERTS, 128), F32)],
        compiler_params=_cparams("arbitrary"),
        name="route",
    )(logits_t, triu)


def _row_copy(src, dst, sem):
    return pltpu.make_async_copy(src, dst, sem)


def _tile_rows(row):
    return pl.ds(pl.multiple_of(row * ROW_TILE, ROW_TILE), ROW_TILE)


ZERO_ROWS = 64


def _dispatch_kernel(cnt_ref, pcnt_ref, poff_ref, nu_ref, xn_ref, route_ref, poffc_ref, xs_hbm,
                     dest_ref, dest_smem, zero_ref, sem_idx, sem_rows):
    i = pl.program_id(0)
    td = route_ref.shape[1]
    n_blocks = xs_hbm.shape[0] // (MOE_ROWS * ROW_TILE)

    rt = route_ref[...]
    ie = lax.broadcasted_iota(I32, (N_EXPERTS, td), 0)
    poffc = poffc_ref[:, 0:1]
    rows = []
    for kk in range(TOP_K):
        base = jnp.sum(jnp.where(ie == rt[kk:kk + 1, :], poffc, 0), axis=0, keepdims=True)
        rows.append(base + rt[TOP_K + kk:TOP_K + kk + 1, :])
    dest_ref[...] = jnp.concatenate(rows + [jnp.zeros((8 - TOP_K, td), I32)], axis=0)
    idx_copy = pltpu.make_async_copy(dest_ref, dest_smem, sem_idx)
    idx_copy.start()

    @pl.when(i == 0)
    def _():
        zero_ref[...] = jnp.zeros_like(zero_ref)

        zero_row = zero_ref.at[pl.ds(0, ROW_TILE)]

        def per_expert(e, carry):
            base = poff_ref[e]

            def start(rw, c):
                _row_copy(zero_row, xs_hbm.at[_tile_rows(base + rw)], sem_rows).start()
                return c

            def wait(rw, c):
                _row_copy(zero_row, xs_hbm.at[_tile_rows(base + rw)], sem_rows).wait()
                return c

            lax.fori_loop(cnt_ref[e], pcnt_ref[e], start, 0)
            lax.fori_loop(cnt_ref[e], pcnt_ref[e], wait, 0)
            return carry

        lax.fori_loop(0, N_EXPERTS, per_expert, 0)

        def unused_block(blk, carry):
            def piece(q):
                start_row = (blk * MOE_ROWS + q * ZERO_ROWS) * ROW_TILE
                return _row_copy(zero_ref,
                                 xs_hbm.at[pl.ds(pl.multiple_of(start_row, ROW_TILE),
                                                 ZERO_ROWS * ROW_TILE)], sem_rows)

            for q in range(MOE_ROWS // ZERO_ROWS):
                piece(q).start()
            for q in range(MOE_ROWS // ZERO_ROWS):
                piece(q).wait()
            return carry

        lax.fori_loop(nu_ref[0], n_blocks, unused_block, 0)

    idx_copy.wait()

    def start(tk, c):
        for kk in range(TOP_K):
            _row_copy(xn_ref.at[_tile_rows(tk)], xs_hbm.at[_tile_rows(dest_smem[kk, tk])],
                      sem_rows).start(priority=kk % 2)
        return c

    lax.fori_loop(0, td, start, 0, unroll=8)
    for _ in range(TOP_K):
        _row_copy(xn_ref, xs_hbm.at[pl.ds(0, td * ROW_TILE)], sem_rows).wait()


def _dispatch(cnt, pcnt, poff, n_used, xn_tiles, route, n_rows):
    t = route.shape[1]
    td = DISPATCH_TILE
    poffc = jnp.broadcast_to(poff[:, None], (N_EXPERTS, LANES))
    return pl.pallas_call(
        _dispatch_kernel,
        grid_spec=pltpu.PrefetchScalarGridSpec(
            num_scalar_prefetch=4,
            grid=(t // td,),
            in_specs=[pl.BlockSpec((td * ROW_TILE, LANES), lambda i, *_: (i, 0)),
                      pl.BlockSpec((8, td), lambda i, *_: (0, i)),
                      pl.BlockSpec((N_EXPERTS, LANES), lambda i, *_: (0, 0))],
            out_specs=[pl.BlockSpec(memory_space=pl.ANY),
                       pl.BlockSpec((8, td), lambda i, *_: (0, i))],
            scratch_shapes=[pltpu.SMEM((8, td), I32),
                            pltpu.VMEM((ZERO_ROWS * ROW_TILE, LANES), F32),
                            pltpu.SemaphoreType.DMA,
                            pltpu.SemaphoreType.DMA]),
        out_shape=[jax.ShapeDtypeStruct((n_rows * ROW_TILE, LANES), F32),
                   jax.ShapeDtypeStruct((8, t), I32)],
        compiler_params=_cparams("arbitrary"),
        name="dispatch",
    )(cnt, pcnt, poff, n_used, xn_tiles, route, poffc)


DEINT = 256


def _expert_kernel(be_ref, nu_ref, xs_ref, w1_ref, b1g_ref, b1l_ref, w2_ref, b2_ref, sel_ref,
                   ys_ref, w1g_ref, w1l_ref, w2b_ref):
    i = pl.program_id(0)
    new_expert = jnp.logical_or(i == 0, be_ref[i] != be_ref[jnp.maximum(i - 1, 0)])

    @pl.when(jnp.logical_and(new_expert, i < nu_ref[0]))
    def _():
        half = DEINT // 2
        for c in range(2 * D_MODEL // DEINT):
            cols = w1_ref[0, :, DEINT * c:DEINT * (c + 1)].astype(BF16)
            sep = jnp.dot(cols, sel_ref[...], preferred_element_type=F32)
            w1g_ref[:, half * c:half * (c + 1)] = sep[:, :half].astype(BF16)
            w1l_ref[:, half * c:half * (c + 1)] = sep[:, half:].astype(BF16)
        w2b_ref[...] = w2_ref[0].astype(BF16)

    @pl.when(i < nu_ref[0])
    def _():
        x = _load_row_tiles(xs_ref, 0, MOE_ROWS).astype(BF16)
        hg = jnp.dot(x, w1g_ref[...], preferred_element_type=F32) + b1g_ref[0]
        hl = jnp.dot(x, w1l_ref[...], preferred_element_type=F32) + b1l_ref[0]
        x_glu = jnp.minimum(hg, SWIGLU_LIMIT)
        x_lin = jnp.clip(hl, -SWIGLU_LIMIT, SWIGLU_LIMIT)
        act = x_glu * jax.nn.sigmoid(SWIGLU_ALPHA * x_glu) * (x_lin + 1.0)
        _store_row_tiles(ys_ref, jnp.dot(act.astype(BF16), w2b_ref[...],
                                         preferred_element_type=F32) + b2_ref[0])

    @pl.when(i >= nu_ref[0])
    def _():
        ys_ref[...] = jnp.zeros_like(ys_ref)


def _experts(blk_exp, n_used, xs, w1, b1g, b1l, w2, b2):
    n_rows = xs.shape[0] // ROW_TILE
    tm = MOE_ROWS
    wmap = lambda i, be, nu: (be[i], 0, 0)
    j = jnp.arange(DEINT, dtype=I32)
    sel = (jnp.where(j % 2 == 0, j // 2, DEINT // 2 + j // 2)[:, None] == j[None, :]).astype(BF16)
    return pl.pallas_call(
        _expert_kernel,
        grid_spec=pltpu.PrefetchScalarGridSpec(
            num_scalar_prefetch=2,
            grid=(n_rows // tm,),
            in_specs=[pl.BlockSpec((tm * ROW_TILE, LANES),
                                   lambda i, be, nu: (jnp.minimum(i, nu[0] - 1), 0)),
                      pl.BlockSpec((1, D_MODEL, 2 * D_MODEL), wmap),
                      pl.BlockSpec((1, 1, D_MODEL), wmap),
                      pl.BlockSpec((1, 1, D_MODEL), wmap),
                      pl.BlockSpec((1, D_MODEL, D_MODEL), wmap),
                      pl.BlockSpec((1, 1, D_MODEL), wmap),
                      pl.BlockSpec((DEINT, DEINT), lambda i, be, nu: (0, 0))],
            out_specs=pl.BlockSpec((tm * ROW_TILE, LANES), lambda i, be, nu: (i, 0)),
            scratch_shapes=[pltpu.VMEM((D_MODEL, D_MODEL), BF16),
                            pltpu.VMEM((D_MODEL, D_MODEL), BF16),
                            pltpu.VMEM((D_MODEL, D_MODEL), BF16)]),
        out_shape=jax.ShapeDtypeStruct((n_rows * ROW_TILE, LANES), F32),
        compiler_params=pltpu.CompilerParams(dimension_semantics=("arbitrary",),
                                             vmem_limit_bytes=EXPERT_VMEM_LIMIT_BYTES),
        name="experts",
    )(blk_exp, n_used, xs, w1, b1g, b1l, w2, b2, sel)


def _combine_kernel(h_ref, gate_ref, nf_ref, dest_hbm, ys_hbm, out_ref,
                    dest_smem, buf_ref, sem_idx, sem_rows):
    i = pl.program_id(0)
    n = pl.num_programs(0)
    tc = h_ref.shape[0]
    slot_rows = TOP_K * tc * ROW_TILE

    def idx_copy(j):
        return pltpu.make_async_copy(dest_hbm.at[:, pl.ds(j * tc, tc)], dest_smem.at[j % 2],
                                     sem_idx.at[j % 2])

    def gather(j):
        s = j % 2

        def start(tk, c):
            for kk in range(TOP_K):
                _row_copy(ys_hbm.at[_tile_rows(dest_smem[s, kk, tk])],
                          buf_ref.at[_tile_rows((s * TOP_K + kk) * tc + tk)],
                          sem_rows.at[s]).start(priority=kk % 2)
            return c

        lax.fori_loop(0, tc, start, 0, unroll=8)

    @pl.when(i == 0)
    def _():
        idx_copy(i).start()
        idx_copy(i).wait()
        gather(i)

        @pl.when(n > 1)
        def _():
            idx_copy(i + 1).start()

    @pl.when(i + 1 < n)
    def _():
        idx_copy(i + 1).wait()
        gather(i + 1)

    @pl.when(i + 2 < n)
    def _():
        idx_copy(i + 2).start()

    cur = i % 2
    for kk in range(TOP_K):
        _row_copy(ys_hbm.at[pl.ds(0, tc * ROW_TILE)],
                  buf_ref.at[pl.ds(pl.multiple_of((cur * TOP_K + kk) * tc * ROW_TILE, ROW_TILE),
                                   tc * ROW_TILE)], sem_rows.at[cur]).wait()

    gate = gate_ref[...]
    acc = jnp.zeros(h_ref.shape, F32)
    for kk in range(TOP_K):
        acc = acc + gate[:, kk:kk + 1] * _load_row_tiles(
            buf_ref, cur * slot_rows + kk * tc * ROW_TILE, tc)
    h = h_ref[...] + acc
    out_ref[...] = h * lax.rsqrt(jnp.mean(h * h, axis=-1, keepdims=True) + RMS_EPS) * nf_ref[...]


def _combine(h, gates_tok, norm_f, dest, ys):
    t = h.shape[0]
    tc = COMBINE_TILE
    return pl.pallas_call(
        _combine_kernel,
        grid=(t // tc,),
        in_specs=[pl.BlockSpec((tc, D_MODEL), lambda i: (i, 0)),
                  pl.BlockSpec((tc, TOP_K), lambda i: (i, 0)),
                  pl.BlockSpec((1, D_MODEL), lambda i: (0, 0)),
                  pl.BlockSpec(memory_space=pl.ANY),
                  pl.BlockSpec(memory_space=pl.ANY)],
        out_specs=pl.BlockSpec((tc, D_MODEL), lambda i: (i, 0)),
        scratch_shapes=[pltpu.SMEM((2, 8, tc), I32),
                        pltpu.VMEM((2 * TOP_K * tc * ROW_TILE, LANES), F32),
                        pltpu.SemaphoreType.DMA((2,)),
                        pltpu.SemaphoreType.DMA((2,))],
        out_shape=jax.ShapeDtypeStruct((t, D_MODEL), F32),
        compiler_params=_cparams("arbitrary"),
        name="combine",
    )(h, gates_tok, norm_f, dest, ys)


def _moe_plan(counts, n_blocks):
    pcnt = ((counts + MOE_ROWS - 1) // MOE_ROWS) * MOE_ROWS
    pend = jnp.cumsum(pcnt)
    poff = pend - pcnt
    n_used = (pend[-1:] // MOE_ROWS).astype(I32)
    blk_start = jnp.arange(n_blocks, dtype=I32) * MOE_ROWS
    blk_exp = jnp.minimum(jnp.sum(pend[None, :] <= blk_start[:, None], axis=1),
                          N_EXPERTS - 1).astype(I32)
    return pcnt.astype(I32), poff.astype(I32), n_used, blk_exp


def kernel(x, norm1_g, w_in, s5_lambda_re, s5_lambda_im, s5_log_dt, s5_b_re, s5_b_im, s5_c_re, s5_c_im, s5_d, s5_w_glu, s5_b_glu, rw_mu, rw_w0, rw_w2, rw_a0, rw_a2, rw_g2, rw_k_k, rw_k_a, rw_r_k, rw_ln_g, rw_ln_b, rw_w_out, w_out, norm2_g, router_w, router_b, moe_w1, moe_b1, moe_w2, moe_b2, norm_f_g):
    bsz, seq, d = x.shape
    t = bsz * seq
    assert d == D_MODEL and norm1_g.shape[0] == 1
    assert seq % S5_STEPS == 0 and seq % RW_CHUNK == 0 and bsz % 8 == 0
    assert t % IN_TILE == 0 and t % MIX_TILE == 0 and t % ROUTE_TILE == 0
    assert t % DISPATCH_TILE == 0 and t % COMBINE_TILE == 0
    x2d = x.reshape(t, d)

    u, prw, gs5, grw = _in_proj(x2d, norm1_g, w_in[0].astype(BF16), bsz, seq)
    a_vec, bd, cd = _s5_params(s5_lambda_re[0], s5_lambda_im[0], s5_log_dt[0], s5_b_re[0],
                               s5_b_im[0], s5_c_re[0], s5_c_im[0])
    s5g = _s5_branch(u.reshape(bsz, seq, S5_WIDTH), gs5.reshape(bsz, seq, D_MODEL),
                     a_vec, bd, cd, s5_d, s5_w_glu[0].astype(BF16), s5_b_glu, bsz, seq)
    zpad = jnp.zeros((64, RW_WIDTH), F32)
    w2p = jnp.concatenate([rw_w2[0], zpad], axis=0).astype(BF16)
    a2p = jnp.concatenate([zpad, rw_a2[0]], axis=0).astype(BF16)
    rwg = _rwkv_branch(prw.reshape(bsz, seq, RW_COLS), grw.reshape(bsz, seq, D_MODEL), rw_mu,
                       rw_w0, w2p, rw_a0, a2p, rw_g2[0].astype(BF16), rw_k_k,
                       rw_k_a, rw_r_k.reshape(1, RW_WIDTH), rw_ln_g, rw_ln_b,
                       rw_w_out[0].astype(BF16), bsz, seq)
    lane_pad = ((0, 0), (0, LANES - N_EXPERTS))
    h, xn2, logits_t = _mix(x2d, s5g.reshape(t, D_MODEL), rwg.reshape(t, D_MODEL),
                            w_out[0].astype(BF16), norm2_g,
                            jnp.pad(router_w[0], lane_pad).astype(BF16),
                            jnp.pad(router_b, lane_pad))

    route, gates, cnt2d = _route(logits_t)
    counts = cnt2d[:, 0]
    n_blocks = (t * TOP_K) // MOE_ROWS + N_EXPERTS
    pcnt, poff, n_used, blk_exp = _moe_plan(counts, n_blocks)
    xs, dest = _dispatch(counts, pcnt, poff, n_used, xn2, route, n_blocks * MOE_ROWS)
    ys = _experts(blk_exp, n_used, xs, moe_w1[0],
                  moe_b1[0][:, None, 0::2], moe_b1[0][:, None, 1::2],
                  moe_w2[0], moe_b2[0][:, None, :])
    out = _combine(h, gates[:TOP_K].T, norm_f_g.reshape(1, D_MODEL), dest, ys)
    return out.reshape(bsz, seq, d)
```
